```python
import jax, jax.numpy as jnp
from jax import lax
import numpy as np

D_MODEL = 1024
BATCH = 2
SEQ = 8192
DEPTH = 2

MIX_WIDTH = D_MODEL
POOL_WIDTH = MIX_WIDTH // 2
POOL_WINDOWS = (2, 4, 8, 16)
N_POOL_GROUPS = len(POOL_WINDOWS)
POOL_GROUP_WIDTH = POOL_WIDTH // N_POOL_GROUPS
ATTN_WIDTH = MIX_WIDTH - POOL_WIDTH
HEAD_DIM = 64
N_HEADS = ATTN_WIDTH // HEAD_DIM
IN_WIDTH = POOL_WIDTH + 3 * ATTN_WIDTH + N_HEADS
D_FF = 4 * D_MODEL
BLOCK_Q = 128
LN_EPS = 1e-5
NEG_INF = -1e30
ALPHA = float((2.0 * DEPTH) ** 0.25)
BETA = float((8.0 * DEPTH) ** -0.25)

kernel_name = "hybrid_pool_fox_postnorm_trunk"


def layer_norm(x, g, b):
    xf = x.astype(jnp.float32)
    mu = jnp.mean(xf, axis=-1, keepdims=True)
    var = jnp.mean(jnp.square(xf - mu), axis=-1, keepdims=True)
    y = (xf - mu) * lax.rsqrt(var + LN_EPS)
    return (y * g.astype(jnp.float32) + b.astype(jnp.float32)).astype(x.dtype)


def pool_mixer(u, w, scale):
    B, S, _ = u.shape
    uf = u.astype(jnp.float32)
    c = jnp.cumsum(uf, axis=1)
    pos = jnp.arange(1, S + 1, dtype=jnp.float32)
    diffs = []
    for g, win in enumerate(POOL_WINDOWS):
        sl = slice(g * POOL_GROUP_WIDTH, (g + 1) * POOL_GROUP_WIDTH)
        cg = c[..., sl]
        shifted = jnp.pad(cg, ((0, 0), (win, 0), (0, 0)))[:, :S]
        count = jnp.minimum(pos, float(win))[None, :, None]
        diffs.append((cg - shifted) / count - uf[..., sl])
    d = jnp.concatenate(diffs, axis=-1).astype(u.dtype)
    d = d.reshape(B, S, N_POOL_GROUPS, POOL_GROUP_WIDTH)
    y = jnp.einsum('bsgc,gcd->bsgd', d, w).reshape(B, S, POOL_WIDTH)
    return y * scale


def forgetting_attention(q, k, v, f_logit):
    B, S, _ = q.shape
    to_heads = lambda t: t.reshape(B, S, N_HEADS, HEAD_DIM).transpose(0, 2, 1, 3)
    q, k, v = to_heads(q), to_heads(k), to_heads(v)
    log_f = jax.nn.log_sigmoid(f_logit.astype(jnp.float32)).transpose(0, 2, 1)
    cf = jnp.cumsum(log_f, axis=-1)
    scale = HEAD_DIM ** -0.5
    kpos = jnp.arange(S)
    n_blocks = S // BLOCK_Q

    def one_block(i):
        start = i * BLOCK_Q
        qb = lax.dynamic_slice_in_dim(q, start, BLOCK_Q, axis=2)
        cq = lax.dynamic_slice_in_dim(cf, start, BLOCK_Q, axis=2)
        qpos = start + jnp.arange(BLOCK_Q)
        s = jnp.einsum('bhqd,bhkd->bhqk', qb, k, preferred_element_type=jnp.float32) * scale
        s = s + cq[..., :, None] - cf[..., None, :]
        s = jnp.where(kpos[None, :] <= qpos[:, None], s, NEG_INF)
        p = jax.nn.softmax(s, axis=-1)
        return jnp.einsum('bhqk,bhkd->bhqd', p.astype(v.dtype), v)

    out = lax.map(one_block, jnp.arange(n_blocks))
    out = out.transpose(1, 0, 3, 2, 4).reshape(B, S, ATTN_WIDTH)
    return out


def hybrid_mixer(x, w_in, b_f, pool_w, pool_scale, w_out):
    h = x @ w_in
    o = 0
    u = h[..., o:o + POOL_WIDTH]; o += POOL_WIDTH
    q = h[..., o:o + ATTN_WIDTH]; o += ATTN_WIDTH
    k = h[..., o:o + ATTN_WIDTH]; o += ATTN_WIDTH
    v = h[..., o:o + ATTN_WIDTH]; o += ATTN_WIDTH
    f_logit = h[..., o:o + N_HEADS] + b_f
    pool_out = pool_mixer(u, pool_w, pool_scale)
    attn_out = forgetting_attention(q, k, v, f_logit)
    return jnp.concatenate([pool_out, attn_out], axis=-1) @ w_out


def setup_inputs(seed: int = 0) -> dict:
    key = jax.random.key(seed)
    ks = jax.random.split(key, 12)
    f32 = jnp.float32
    x = jax.random.normal(ks[0], (BATCH, SEQ, D_MODEL), f32)
    w_in = jax.random.normal(ks[1], (DEPTH, D_MODEL, IN_WIDTH), f32) * D_MODEL ** -0.5
    col_scale = np.ones((IN_WIDTH,), np.float32)
    col_scale[:POOL_WIDTH] = BETA
    col_scale[POOL_WIDTH + 2 * ATTN_WIDTH:POOL_WIDTH + 3 * ATTN_WIDTH] = BETA
    w_in = w_in * jnp.asarray(col_scale)
    b_f = 3.0 + 0.5 * jax.random.normal(ks[2], (DEPTH, N_HEADS), f32)
    pool_w = jax.random.normal(ks[3], (DEPTH, N_POOL_GROUPS, POOL_GROUP_WIDTH, POOL_GROUP_WIDTH), f32) * POOL_GROUP_WIDTH ** -0.5
    pool_scale = 1.0 + 0.1 * jax.random.normal(ks[4], (DEPTH, POOL_WIDTH), f32)
    w_out = jax.random.normal(ks[5], (DEPTH, MIX_WIDTH, D_MODEL), f32) * (MIX_WIDTH ** -0.5 * BETA)
    ln1_g = 1.0 + 0.05 * jax.random.normal(ks[6], (DEPTH, D_MODEL), f32)
    ln1_b = 0.02 * jax.random.normal(ks[7], (DEPTH, D_MODEL), f32)
    w_mlp1 = jax.random.normal(ks[8], (DEPTH, D_MODEL, D_FF), f32) * D_MODEL ** -0.5
    w_mlp2 = jax.random.normal(ks[9], (DEPTH, D_FF, D_MODEL), f32) * (D_FF ** -0.5 * BETA)
    ln2_g = 1.0 + 0.05 * jax.random.normal(ks[10], (DEPTH, D_MODEL), f32)
    ln2_b = 0.02 * jax.random.normal(ks[11], (DEPTH, D_MODEL), f32)
    return {"x": x, "w_in": w_in, "b_f": b_f, "pool_w": pool_w, "pool_scale": pool_scale,
            "w_out": w_out, "ln1_g": ln1_g, "ln1_b": ln1_b, "w_mlp1": w_mlp1,
            "w_mlp2": w_mlp2, "ln2_g": ln2_g, "ln2_b": ln2_b}


def reference(x, w_in, b_f, pool_w, pool_scale, w_out, ln1_g, ln1_b, w_mlp1, w_mlp2, ln2_g, ln2_b):
    for l in range(DEPTH):
        mix = hybrid_mixer(x, w_in[l], b_f[l], pool_w[l], pool_scale[l], w_out[l])
        x = layer_norm(ALPHA * x + mix, ln1_g[l], ln1_b[l])
        hid = jnp.square(jax.nn.relu(x @ w_mlp1[l]))
        x = layer_norm(ALPHA * x + hid @ w_mlp2[l], ln2_g[l], ln2_b[l])
    return x
```

```python
import functools

import jax
import jax.numpy as jnp
import numpy as np
from jax import lax
from jax.experimental import pallas as pl
from jax.experimental.pallas import tpu as pltpu

D_MODEL = 1024
DEPTH = 2
POOL_WIDTH = 512
POOL_WINDOWS = (2, 4, 8, 16)
POOL_GROUP_WIDTH = 128
ATTN_WIDTH = 512
HEAD_DIM = 64
N_HEADS = 8
N_PAIRS = N_HEADS // 2
D_FF = 4 * D_MODEL
LN_EPS = 1e-5
NEG_INF = -1e30
ALPHA = float((2.0 * DEPTH) ** 0.25)

LANES = 128
POOL_HALO = 32
GATE_ONE_LANE = 24
GATE_COLS_PER_HEAD = 6

TM_IN = 512
T_ATT = 512
TM_POST = 512
FF_CHUNK = 1024
VMEM_LIMIT = 56 * 1024 * 1024

_BF16 = jnp.bfloat16
_F32 = jnp.float32


def _dot(a, b):
    return jnp.dot(a, b, preferred_element_type=_F32)


def _split3(x):
    hi = x.astype(_BF16)
    r = x - hi.astype(_F32)
    mid = r.astype(_BF16)
    lo = (r - mid.astype(_F32)).astype(_BF16)
    return hi, mid, lo


def _inproj_kernel(x_ref, wu_ref, wq_ref, wk_ref, wv_ref, wf_ref, bf_ref, pw_ref, ps_ref,
                   eq_ref, ek_ref,
                   pool_ref, q_ref, k_ref, v0_ref, v1_ref, gq_ref, gk_ref,
                   ubuf, sb1, sb2, sb3, carry):
    i = pl.program_id(1)
    tm = x_ref.shape[0]
    hl = POOL_HALO

    @pl.when(i == 0)
    def _():
        ubuf[0:hl, :] = jnp.zeros((hl, POOL_WIDTH), _F32)
        carry[...] = jnp.zeros(carry.shape, _F32)

    @pl.when(i > 0)
    def _():
        ubuf[0:hl, :] = ubuf[tm:tm + hl, :]

    xb = x_ref[...].astype(_BF16)
    q_ref[...] = _dot(xb, wq_ref[...]).astype(_BF16)
    k_ref[...] = _dot(xb, wk_ref[...]).astype(_BF16)
    v = _dot(xb, wv_ref[...])
    lane_v = lax.broadcasted_iota(jnp.int32, v.shape, 1) % LANES
    v0_ref[...] = jnp.where(lane_v < HEAD_DIM, v,
                            jnp.where(lane_v == HEAD_DIM, 1.0, 0.0)).astype(_BF16)
    v1_ref[...] = jnp.where(lane_v >= HEAD_DIM, v,
                            jnp.where(lane_v == 0, 1.0, 0.0)).astype(_BF16)

    lane = lax.broadcasted_iota(jnp.int32, (tm, LANES), 1)
    f = _dot(xb, wf_ref[...]) + bf_ref[...]
    lf = jnp.minimum(f, 0.0) - jnp.log1p(jnp.exp(-jnp.abs(f)))
    lf = jnp.where(lane < GATE_ONE_LANE, lf, 0.0)
    row = lax.broadcasted_iota(jnp.int32, (tm, tm), 0)
    col = lax.broadcasted_iota(jnp.int32, (tm, tm), 1)
    tri = jnp.where(col <= row, 1.0, 0.0).astype(_BF16)
    l_hi, l_mid, l_lo = _split3(lf)
    c = _dot(tri, l_hi) + _dot(tri, l_mid) + _dot(tri, l_lo) + carry[7:8, :]
    carry[...] = c[tm - 8:tm, :]
    c_hi, c_mid, c_lo = _split3(c)
    one = jnp.where(lane == GATE_ONE_LANE, 1.0, 0.0).astype(_BF16)
    gate = jnp.where(lane < 8, c_hi, jnp.where(lane < 16, c_mid, jnp.where(lane < 24, c_lo, one)))
    gq_ref[...] = _dot(gate, eq_ref[...]).astype(_BF16)
    gk_ref[...] = _dot(gate, ek_ref[...]).astype(_BF16)

    ubuf[hl:hl + tm, :] = _dot(xb, wu_ref[...])
    s2 = ubuf[8:tm + 32, :] + ubuf[7:tm + 31, :]
    sb1[8:tm + 32, :] = s2[:, 128:512]
    s4 = sb1[16:tm + 32, :] + sb1[14:tm + 30, :]
    sb2[16:tm + 32, :] = s4[:, 128:384]
    s8 = sb2[24:tm + 32, :] + sb2[20:tm + 28, :]
    sb3[24:tm + 32, :] = s8[:, 128:256]
    s16 = sb3[32:tm + 32, :] + sb3[24:tm + 24, :]
    sums = (s2[24:, 0:128], s4[16:, 0:128], s8[8:, 0:128], s16)
    pos = (i * tm + lax.broadcasted_iota(jnp.int32, (tm, 1), 0) + 1).astype(_F32)
    for g, win in enumerate(POOL_WINDOWS):
        sl = slice(g * POOL_GROUP_WIDTH, (g + 1) * POOL_GROUP_WIDTH)
        cnt = jnp.minimum(pos, float(win))
        d = sums[g] / cnt - ubuf[hl:hl + tm, sl]
        y = _dot(d.astype(_BF16), pw_ref[g]) * ps_ref[:, sl]
        pool_ref[:, sl] = y.astype(_BF16)


def _inproj(x, wu, wq, wk, wv, wf, bf, pw, ps, eq, ek):
    b, s, d = x.shape
    tm = TM_IN
    const = lambda shape: pl.BlockSpec(shape, lambda bi, i: (0,) * len(shape))
    tile = lambda w: pl.BlockSpec((None, tm, w), lambda bi, i: (bi, i, 0))
    out_shape = [jax.ShapeDtypeStruct((b, s, 512), _BF16) for _ in range(7)]
    return pl.pallas_call(
        _inproj_kernel,
        grid=(b, s // tm),
        in_specs=[tile(d), const(wu.shape), const(wq.shape), const(wk.shape), const(wv.shape),
                  const(wf.shape), const(bf.shape), const(pw.shape), const(ps.shape),
                  const(eq.shape), const(ek.shape)],
        out_specs=[tile(512) for _ in range(7)],
        out_shape=out_shape,
        scratch_shapes=[pltpu.VMEM((tm + POOL_HALO, 512), _F32),
                        pltpu.VMEM((tm + POOL_HALO, 384), _F32),
                        pltpu.VMEM((tm + POOL_HALO, 256), _F32),
                        pltpu.VMEM((tm + POOL_HALO, 128), _F32),
                        pltpu.VMEM((8, LANES), _F32)],
        compiler_params=pltpu.CompilerParams(
            dimension_semantics=("arbitrary", "arbitrary"), vmem_limit_bytes=VMEM_LIMIT),
        name="inproj",
    )(x, wu, wq, wk, wv, wf, bf, pw, ps, eq, ek)


def _attn_kernel(q_ref, gq_ref, k_ref, gk_ref, v0_ref, v1_ref, o_ref, m_sc, acc_sc):
    i = pl.program_id(2)
    tq = q_ref.shape[0]
    tk = T_ATT
    lane = lax.broadcasted_iota(jnp.int32, (tq, LANES), 1)
    qp = q_ref[...].astype(_F32)
    gqp = gq_ref[...].astype(_F32)
    g = GATE_COLS_PER_HEAD
    qa = (jnp.concatenate([jnp.where(lane < HEAD_DIM, qp, 0.0),
                           jnp.where(lane < g, gqp, 0.0)], axis=1).astype(_BF16),
          jnp.concatenate([jnp.where(lane >= HEAD_DIM, qp, 0.0),
                           jnp.where((lane >= g) & (lane < 2 * g), gqp, 0.0)], axis=1).astype(_BF16))

    m_sc[...] = jnp.full(m_sc.shape, NEG_INF, _F32)
    acc_sc[...] = jnp.zeros(acc_sc.shape, _F32)

    def step(start, masked):
        ka = jnp.concatenate([k_ref[pl.ds(start, tk), :], gk_ref[pl.ds(start, tk), :]], axis=1)
        va = (v0_ref[pl.ds(start, tk), :], v1_ref[pl.ds(start, tk), :])
        for h in range(2):
            s = lax.dot_general(qa[h], ka, (((1,), (1,)), ((), ())),
                                preferred_element_type=_F32)
            if masked:
                r = lax.broadcasted_iota(jnp.int32, (tq, tk), 0)
                c = lax.broadcasted_iota(jnp.int32, (tq, tk), 1)
                s = jnp.where(c <= r, s, NEG_INF)
            m_prev = m_sc[h]
            m_new = jnp.maximum(m_prev, jnp.max(s, axis=1, keepdims=True))
            alpha = jnp.exp(m_prev - m_new)
            p = jnp.exp(s - m_new)
            acc_sc[h] = alpha * acc_sc[h] + _dot(p.astype(_BF16), va[h])
            m_sc[h] = m_new

    def body(j, carry):
        step(pl.multiple_of(j * tk, tk), False)
        return carry

    lax.fori_loop(0, i, body, 0)
    step(pl.multiple_of(i * tk, tk), True)

    acc0 = acc_sc[0]
    acc1 = acc_sc[1]
    out = jnp.where(lane < HEAD_DIM, acc0 / acc0[:, HEAD_DIM:HEAD_DIM + 1], acc1 / acc1[:, 0:1])
    o_ref[...] = out.astype(o_ref.dtype)


def _attention(q, gq, k, gk, v0, v1):
    b, s, _ = q.shape
    t = T_ATT
    qspec = pl.BlockSpec((None, t, LANES), lambda bi, p, i: (bi, i, p))
    kspec = pl.BlockSpec((None, s, LANES), lambda bi, p, i: (bi, 0, p))
    return pl.pallas_call(
        _attn_kernel,
        grid=(b, N_PAIRS, s // t),
        in_specs=[qspec, qspec, kspec, kspec, kspec, kspec],
        out_specs=qspec,
        out_shape=jax.ShapeDtypeStruct((b, s, ATTN_WIDTH), _BF16),
        scratch_shapes=[pltpu.VMEM((2, t, 1), _F32), pltpu.VMEM((2, t, LANES), _F32)],
        compiler_params=pltpu.CompilerParams(
            dimension_semantics=("arbitrary", "arbitrary", "arbitrary"),
            vmem_limit_bytes=VMEM_LIMIT),
        name="fox_attn",
    )(q, gq, k, gk, v0, v1)


def _layer_norm(z, g, b):
    mu = jnp.mean(z, axis=-1, keepdims=True)
    zc = z - mu
    var = jnp.mean(zc * zc, axis=-1, keepdims=True)
    return zc * lax.rsqrt(var + LN_EPS) * g + b


def _post_kernel(x_ref, pool_ref, attn_ref, wop_ref, woa_ref, g1_ref, b1_ref,
                 w1_ref, w2_ref, g2_ref, b2_ref, o_ref):
    mix = _dot(pool_ref[...], wop_ref[...]) + _dot(attn_ref[...], woa_ref[...])
    x1 = _layer_norm(ALPHA * x_ref[...] + mix, g1_ref[...], b1_ref[...])
    x1b = x1.astype(_BF16)
    y = jnp.zeros_like(x1)
    for c in range(D_FF // FF_CHUNK):
        sl = slice(c * FF_CHUNK, (c + 1) * FF_CHUNK)
        hid = jnp.maximum(_dot(x1b, w1_ref[:, sl]), 0.0)
        y = y + _dot((hid * hid).astype(_BF16), w2_ref[sl, :])
    o_ref[...] = _layer_norm(ALPHA * x1 + y, g2_ref[...], b2_ref[...])


def _post(x, pool, attn, wop, woa, g1, b1, w1, w2, g2, b2):
    t, d = x.shape
    tm = TM_POST
    const = lambda a: pl.BlockSpec(a.shape, lambda i: (0,) * a.ndim, pipeline_mode=pl.Buffered(1))
    tile = lambda w: pl.BlockSpec((tm, w), lambda i: (i, 0))
    return pl.pallas_call(
        _post_kernel,
        grid=(t // tm,),
        in_specs=[tile(d), tile(512), tile(512), const(wop), const(woa), const(g1), const(b1),
                  const(w1), const(w2), const(g2), const(b2)],
        out_specs=tile(d),
        out_shape=jax.ShapeDtypeStruct((t, d), _F32),
        compiler_params=pltpu.CompilerParams(
            dimension_semantics=("arbitrary",), vmem_limit_bytes=VMEM_LIMIT),
        name="post",
    )(x, pool, attn, wop, woa, g1, b1, w1, w2, g2, b2)


def _gate_placement():
    eq = np.zeros((LANES, N_PAIRS * LANES), np.float32)
    ek = np.zeros((LANES, N_PAIRS * LANES), np.float32)
    for h in range(N_HEADS):
        base = (h // 2) * LANES + (h % 2) * GATE_COLS_PER_HEAD
        for piece in range(3):
            eq[8 * piece + h, base + piece] = 1.0
            eq[GATE_ONE_LANE, base + 3 + piece] = 1.0
            ek[GATE_ONE_LANE, base + piece] = 1.0
            ek[8 * piece + h, base + 3 + piece] = -1.0
    return jnp.asarray(eq, _BF16), jnp.asarray(ek, _BF16)


def kernel(x, w_in, b_f, pool_w, pool_scale, w_out, ln1_g, ln1_b, w_mlp1, w_mlp2, ln2_g, ln2_b):
    b, s, d = x.shape
    eq, ek = _gate_placement()
    o_q = POOL_WIDTH
    o_k = o_q + ATTN_WIDTH
    o_v = o_k + ATTN_WIDTH
    o_f = o_v + ATTN_WIDTH
    row = lambda a: a.reshape(1, -1)
    for l in range(DEPTH):
        w = w_in[l]
        wu = w[:, :o_q].astype(_BF16)
        wq = (w[:, o_q:o_k] * HEAD_DIM ** -0.5).astype(_BF16)
        wk = w[:, o_k:o_v].astype(_BF16)
        wv = w[:, o_v:o_f].astype(_BF16)
        wf = jnp.pad(jnp.tile(w[:, o_f:], (1, 3)), ((0, 0), (0, LANES - 3 * N_HEADS))).astype(_BF16)
        bf = jnp.pad(jnp.tile(b_f[l], 3), (0, LANES - 3 * N_HEADS)).reshape(1, LANES)
        pool, q, k, v0, v1, gq, gk = _inproj(x, wu, wq, wk, wv, wf, bf, pool_w[l].astype(_BF16),
                                             row(pool_scale[l]), eq, ek)
        attn = _attention(q, gq, k, gk, v0, v1)
        wo = w_out[l].astype(_BF16)
        x = _post(x.reshape(b * s, d), pool.reshape(b * s, -1), attn.reshape(b * s, -1),
                  wo[:POOL_WIDTH], wo[POOL_WIDTH:], row(ln1_g[l]), row(ln1_b[l]),
                  w_mlp1[l].astype(_BF16), w_mlp2[l].astype(_BF16),
                  row(ln2_g[l]), row(ln2_b[l])).reshape(b, s, d)
    return x
```

```python
import jax
import jax.numpy as jnp
import numpy as np
from jax import lax
from jax.experimental import pallas as pl
from jax.experimental.pallas import tpu as pltpu

D_MODEL = 1024
DEPTH = 2
POOL_WIDTH = 512
POOL_WINDOWS = (2, 4, 8, 16)
POOL_GROUP_WIDTH = 128
ATTN_WIDTH = 512
HEAD_DIM = 64
N_HEADS = 8
N_PAIRS = N_HEADS // 2
D_FF = 4 * D_MODEL
LN_EPS = 1e-5
NEG_INF = -1e30
ALPHA = float((2.0 * DEPTH) ** 0.25)

LANES = 128
MXU_DIM = 256
POOL_HALO = 32
GATE_ONE_LANE = 24
GATE_COLS_PER_HEAD = 6
PAIR_WIDTH = 2 * LANES
VT_ROWS = 80

T_ATT = 512
TM_IN = T_ATT
TM_POST = 512
FF_CHUNK = 1024
VMEM_LIMIT = 56 * 1024 * 1024

_BF16 = jnp.bfloat16
_F32 = jnp.float32
_NT = (((1,), (1,)), ((), ()))


def _dot(a, b):
    return jnp.dot(a, b, preferred_element_type=_F32)


def _dot_nt(a, b):
    return lax.dot_general(a, b, _NT, preferred_element_type=_F32)


def _split3(x):
    hi = x.astype(_BF16)
    r = x - hi.astype(_F32)
    mid = r.astype(_BF16)
    lo = (r - mid.astype(_F32)).astype(_BF16)
    return hi, mid, lo


def _inproj_kernel(x_ref, wu_ref, wq_ref, wk_ref, wvt_ref, wf_ref, bf_ref, pw_ref, ps_ref,
                   eq_ref, ek_ref,
                   pool_ref, qa_ref, ka_ref, vt_ref,
                   ubuf, sb1, sb2, sb3, carry):
    i = pl.program_id(1)
    tm = x_ref.shape[0]
    hl = POOL_HALO

    @pl.when(i == 0)
    def _():
        ubuf[0:hl, :] = jnp.zeros((hl, POOL_WIDTH), _F32)
        carry[...] = jnp.zeros(carry.shape, _F32)

    @pl.when(i > 0)
    def _():
        ubuf[0:hl, :] = ubuf[tm:tm + hl, :]

    xb = x_ref[...].astype(_BF16)

    lane = lax.broadcasted_iota(jnp.int32, (tm, LANES), 1)
    f = _dot(xb, wf_ref[...]) + bf_ref[...]
    lf = jnp.minimum(f, 0.0) - jnp.log1p(jnp.exp(-jnp.abs(f)))
    lf = jnp.where(lane < GATE_ONE_LANE, lf, 0.0)
    row = lax.broadcasted_iota(jnp.int32, (tm, tm), 0)
    col = lax.broadcasted_iota(jnp.int32, (tm, tm), 1)
    tri = jnp.where(col <= row, 1.0, 0.0).astype(_BF16)
    l_hi, l_mid, l_lo = _split3(lf)
    c = _dot(tri, l_hi) + _dot(tri, l_mid) + _dot(tri, l_lo) + carry[7:8, :]
    carry[...] = c[tm - 8:tm, :]
    c_hi, c_mid, c_lo = _split3(c)
    one = jnp.where(lane == GATE_ONE_LANE, 1.0, 0.0).astype(_BF16)
    gate = jnp.where(lane < 8, c_hi, jnp.where(lane < 16, c_mid, jnp.where(lane < 24, c_lo, one)))
    gq = _dot(gate, eq_ref[...]).astype(_BF16)
    gk = _dot(gate, ek_ref[...]).astype(_BF16)

    q = _dot(xb, wq_ref[...]).astype(_BF16)
    k = _dot(xb, wk_ref[...]).astype(_BF16)
    for p in range(N_PAIRS):
        src = slice(p * LANES, (p + 1) * LANES)
        qa_ref[:, p * PAIR_WIDTH:p * PAIR_WIDTH + LANES] = q[:, src]
        qa_ref[:, p * PAIR_WIDTH + LANES:(p + 1) * PAIR_WIDTH] = gq[:, src]
        ka_ref[:, p * PAIR_WIDTH:p * PAIR_WIDTH + LANES] = k[:, src]
        ka_ref[:, p * PAIR_WIDTH + LANES:(p + 1) * PAIR_WIDTH] = gk[:, src]

    vt = _dot_nt(wvt_ref[...], xb)
    pad_rows = VT_ROWS - HEAD_DIM
    ones_row = jnp.where(lax.broadcasted_iota(jnp.int32, (pad_rows, tm), 0) == 0, 1.0, 0.0)
    for h in range(N_HEADS):
        vt_ref[h, 0:HEAD_DIM, :] = vt[h * HEAD_DIM:(h + 1) * HEAD_DIM, :].astype(_BF16)
        vt_ref[h, HEAD_DIM:VT_ROWS, :] = ones_row.astype(_BF16)

    ubuf[hl:hl + tm, :] = _dot(xb, wu_ref[...])
    s2 = ubuf[8:tm + 32, :] + ubuf[7:tm + 31, :]
    sb1[8:tm + 32, :] = s2[:, 128:512]
    s4 = sb1[16:tm + 32, :] + sb1[14:tm + 30, :]
    sb2[16:tm + 32, :] = s4[:, 128:384]
    s8 = sb2[24:tm + 32, :] + sb2[20:tm + 28, :]
    sb3[24:tm + 32, :] = s8[:, 128:256]
    s16 = sb3[32:tm + 32, :] + sb3[24:tm + 24, :]
    sums = (s2[24:, 0:128], s4[16:, 0:128], s8[8:, 0:128], s16)
    pos = (i * tm + lax.broadcasted_iota(jnp.int32, (tm, 1), 0) + 1).astype(_F32)
    for g, win in enumerate(POOL_WINDOWS):
        sl = slice(g * POOL_GROUP_WIDTH, (g + 1) * POOL_GROUP_WIDTH)
        cnt = jnp.minimum(pos, float(win))
        d = sums[g] / cnt - ubuf[hl:hl + tm, sl]
        y = _dot(d.astype(_BF16), pw_ref[g]) * ps_ref[:, sl]
        pool_ref[:, sl] = y.astype(_BF16)


def _inproj(x, wu, wq, wk, wvt, wf, bf, pw, ps, eq, ek):
    b, s, d = x.shape
    tm = TM_IN
    const = lambda shape: pl.BlockSpec(shape, lambda bi, i: (0,) * len(shape))
    tile = lambda w: pl.BlockSpec((None, tm, w), lambda bi, i: (bi, i, 0))
    out_shape = [jax.ShapeDtypeStruct((b, s, POOL_WIDTH), _BF16),
                 jax.ShapeDtypeStruct((b, s, N_PAIRS * PAIR_WIDTH), _BF16),
                 jax.ShapeDtypeStruct((b, s, N_PAIRS * PAIR_WIDTH), _BF16),
                 jax.ShapeDtypeStruct((b, N_HEADS, s // tm, VT_ROWS, tm), _BF16)]
    out_specs = [tile(POOL_WIDTH), tile(N_PAIRS * PAIR_WIDTH), tile(N_PAIRS * PAIR_WIDTH),
                 pl.BlockSpec((None, N_HEADS, None, VT_ROWS, tm), lambda bi, i: (bi, 0, i, 0, 0))]
    return pl.pallas_call(
        _inproj_kernel,
        grid=(b, s // tm),
        in_specs=[tile(d), const(wu.shape), const(wq.shape), const(wk.shape), const(wvt.shape),
                  const(wf.shape), const(bf.shape), const(pw.shape), const(ps.shape),
                  const(eq.shape), const(ek.shape)],
        out_specs=out_specs,
        out_shape=out_shape,
        scratch_shapes=[pltpu.VMEM((tm + POOL_HALO, 512), _F32),
                        pltpu.VMEM((tm + POOL_HALO, 384), _F32),
                        pltpu.VMEM((tm + POOL_HALO, 256), _F32),
                        pltpu.VMEM((tm + POOL_HALO, 128), _F32),
                        pltpu.VMEM((8, LANES), _F32)],
        compiler_params=pltpu.CompilerParams(
            dimension_semantics=("arbitrary", "arbitrary"), vmem_limit_bytes=VMEM_LIMIT),
        name="inproj",
    )(x, wu, wq, wk, wvt, wf, bf, pw, ps, eq, ek)


def _attn_kernel(qa_ref, ka_ref, vt_ref, o_ref, m_sc, acc_sc):
    i = pl.program_id(2)
    tq = qa_ref.shape[0]
    tk = T_ATT
    g = GATE_COLS_PER_HEAD
    lane = lax.broadcasted_iota(jnp.int32, (tq, PAIR_WIDTH), 1)
    qf = qa_ref[...].astype(_F32)
    keep = ((lane < HEAD_DIM) | ((lane >= LANES) & (lane < LANES + g)),
            ((lane >= HEAD_DIM) & (lane < LANES)) | ((lane >= LANES + g) & (lane < LANES + 2 * g)))
    qh = [jnp.where(kp, qf, 0.0).astype(_BF16) for kp in keep]

    m_sc[...] = jnp.full(m_sc.shape, NEG_INF, _F32)
    acc_sc[...] = jnp.zeros(acc_sc.shape, _F32)

    def step(j, masked):
        kb = ka_ref[pl.ds(pl.multiple_of(j * tk, tk), tk), :]
        units = [(h, c) for h in range(2) for c in range(tq // MXU_DIM)]

        def scores(h, c):
            return _dot_nt(kb, qh[h][c * MXU_DIM:(c + 1) * MXU_DIM, :])

        s_next = scores(*units[0])
        for n, (h, c) in enumerate(units):
            s = s_next
            if n + 1 < len(units):
                s_next = scores(*units[n + 1])
            cs = slice(c * MXU_DIM, (c + 1) * MXU_DIM)
            if masked:
                kpos = lax.broadcasted_iota(jnp.int32, s.shape, 0)
                qpos = lax.broadcasted_iota(jnp.int32, s.shape, 1) + c * MXU_DIM
                s = jnp.where(kpos <= qpos, s, NEG_INF)
            m_prev = m_sc[h, :, cs]
            m_new = jnp.maximum(m_prev, jnp.max(s, axis=0, keepdims=True))
            alpha = jnp.exp(m_prev - m_new)
            p = jnp.exp(s - m_new).astype(_BF16)
            acc_sc[h, :, cs] = alpha * acc_sc[h, :, cs] + _dot(vt_ref[h, j], p)
            m_sc[h, :, cs] = m_new

    def body(j, carry):
        step(j, False)
        return carry

    lax.fori_loop(0, i, body, 0)
    step(i, True)

    outs = []
    for h in range(2):
        a = acc_sc[h]
        outs.append(a[0:HEAD_DIM, :] / a[HEAD_DIM:HEAD_DIM + 1, :])
    o_ref[...] = jnp.concatenate(outs, axis=0).T.astype(o_ref.dtype)


def _attention(qa, ka, vt):
    b, s, _ = qa.shape
    t = T_ATT
    return pl.pallas_call(
        _attn_kernel,
        grid=(b, N_PAIRS, s // t),
        in_specs=[pl.BlockSpec((None, t, PAIR_WIDTH), lambda bi, p, i: (bi, i, p)),
                  pl.BlockSpec((None, s, PAIR_WIDTH), lambda bi, p, i: (bi, 0, p)),
                  pl.BlockSpec((None, 2, s // t, VT_ROWS, t), lambda bi, p, i: (bi, p, 0, 0, 0))],
        out_specs=pl.BlockSpec((None, t, LANES), lambda bi, p, i: (bi, i, p)),
        out_shape=jax.ShapeDtypeStruct((b, s, ATTN_WIDTH), _BF16),
        scratch_shapes=[pltpu.VMEM((2, 1, t), _F32), pltpu.VMEM((2, VT_ROWS, t), _F32)],
        compiler_params=pltpu.CompilerParams(
            dimension_semantics=("arbitrary", "arbitrary", "arbitrary"),
            vmem_limit_bytes=VMEM_LIMIT),
        name="fox_attn",
    )(qa, ka, vt)


def _layer_norm(z, g, b):
    mu = jnp.mean(z, axis=-1, keepdims=True)
    zc = z - mu
    var = jnp.mean(zc * zc, axis=-1, keepdims=True)
    return zc * lax.rsqrt(var + LN_EPS) * g + b


def _post_kernel(x_ref, pool_ref, attn_ref, wop_ref, woa_ref, g1_ref, b1_ref,
                 w1_ref, w2_ref, g2_ref, b2_ref, o_ref):
    mix = _dot(pool_ref[...], wop_ref[...]) + _dot(attn_ref[...], woa_ref[...])
    x1 = _layer_norm(ALPHA * x_ref[...] + mix, g1_ref[...], b1_ref[...])
    x1b = x1.astype(_BF16)
    y = jnp.zeros_like(x1)
    for c in range(D_FF // FF_CHUNK):
        sl = slice(c * FF_CHUNK, (c + 1) * FF_CHUNK)
        hid = jnp.maximum(_dot(x1b, w1_ref[:, sl]), 0.0)
        y = y + _dot((hid * hid).astype(_BF16), w2_ref[sl, :])
    o_ref[...] = _layer_norm(ALPHA * x1 + y, g2_ref[...], b2_ref[...])


def _post(x, pool, attn, wop, woa, g1, b1, w1, w2, g2, b2):
    t, d = x.shape
    tm = TM_POST
    const = lambda a: pl.BlockSpec(a.shape, lambda i: (0,) * a.ndim, pipeline_mode=pl.Buffered(1))
    tile = lambda w: pl.BlockSpec((tm, w), lambda i: (i, 0))
    return pl.pallas_call(
        _post_kernel,
        grid=(t // tm,),
        in_specs=[tile(d), tile(512), tile(512), const(wop), const(woa), const(g1), const(b1),
                  const(w1), const(w2), const(g2), const(b2)],
        out_specs=tile(d),
        out_shape=jax.ShapeDtypeStruct((t, d), _F32),
        compiler_params=pltpu.CompilerParams(
            dimension_semantics=("arbitrary",), vmem_limit_bytes=VMEM_LIMIT),
        name="post",
    )(x, pool, attn, wop, woa, g1, b1, w1, w2, g2, b2)


def _gate_placement():
    eq = np.zeros((LANES, N_PAIRS * LANES), np.float32)
    ek = np.zeros((LANES, N_PAIRS * LANES), np.float32)
    for h in range(N_HEADS):
        base = (h // 2) * LANES + (h % 2) * GATE_COLS_PER_HEAD
        for piece in range(3):
            eq[8 * piece + h, base + piece] = 1.0
            eq[GATE_ONE_LANE, base + 3 + piece] = 1.0
            ek[GATE_ONE_LANE, base + piece] = 1.0
            ek[8 * piece + h, base + 3 + piece] = -1.0
    return jnp.asarray(eq, _BF16), jnp.asarray(ek, _BF16)


def kernel(x, w_in, b_f, pool_w, pool_scale, w_out, ln1_g, ln1_b, w_mlp1, w_mlp2, ln2_g, ln2_b):
    b, s, d = x.shape
    assert s % T_ATT == 0 and (b * s) % TM_POST == 0
    eq, ek = _gate_placement()
    o_q = POOL_WIDTH
    o_k = o_q + ATTN_WIDTH
    o_v = o_k + ATTN_WIDTH
    o_f = o_v + ATTN_WIDTH
    row = lambda a: a.reshape(1, -1)
    for l in range(DEPTH):
        w = w_in[l]
        wu = w[:, :o_q].astype(_BF16)
        wq = (w[:, o_q:o_k] * HEAD_DIM ** -0.5).astype(_BF16)
        wk = w[:, o_k:o_v].astype(_BF16)
        wvt = w[:, o_v:o_f].T.astype(_BF16)
        wf = jnp.pad(jnp.tile(w[:, o_f:], (1, 3)), ((0, 0), (0, LANES - 3 * N_HEADS))).astype(_BF16)
        bf = jnp.pad(jnp.tile(b_f[l], 3), (0, LANES - 3 * N_HEADS)).reshape(1, LANES)
        pool, qa, ka, vt = _inproj(x, wu, wq, wk, wvt, wf, bf, pool_w[l].astype(_BF16),
                                   row(pool_scale[l]), eq, ek)
        attn = _attention(qa, ka, vt)
        wo = w_out[l].astype(_BF16)
        x = _post(x.reshape(b * s, d), pool.reshape(b * s, -1), attn.reshape(b * s, -1),
                  wo[:POOL_WIDTH], wo[POOL_WIDTH:], row(ln1_g[l]), row(ln1_b[l]),
                  w_mlp1[l].astype(_BF16), w_mlp2[l].astype(_BF16),
                  row(ln2_g[l]), row(ln2_b[l])).reshape(b, s, d)
    return x
```

```python
import jax
import jax.numpy as jnp
import numpy as np
from jax import lax
from jax.experimental import pallas as pl
from jax.experimental.pallas import tpu as pltpu

D_MODEL = 1024
DEPTH = 2
POOL_WIDTH = 512
POOL_WINDOWS = (2, 4, 8, 16)
POOL_GROUP_WIDTH = 128
ATTN_WIDTH = 512
HEAD_DIM = 64
N_HEADS = 8
N_PAIRS = N_HEADS // 2
D_FF = 4 * D_MODEL
LN_EPS = 1e-5
NEG_INF = -1e30
ALPHA = float((2.0 * DEPTH) ** 0.25)

LANES = 128
MXU_DIM = 256
POOL_HALO = 32
GATE_ONE_LANE = 24
GATE_COLS_PER_HEAD = 6
PAIR_WIDTH = 2 * LANES
VT_ROWS = 80

T_Q = 1024
T_K = 512
ATT_RING = 4
ATT_LOOKAHEAD = 2
TM_IN = T_K
TM_POST = 512
FF_CHUNK = 1024
VMEM_LIMIT = 56 * 1024 * 1024

_BF16 = jnp.bfloat16
_F32 = jnp.float32
_NT = (((1,), (1,)), ((), ()))


def _dot(a, b):
    return jnp.dot(a, b, preferred_element_type=_F32)


def _dot_nt(a, b):
    return lax.dot_general(a, b, _NT, preferred_element_type=_F32)


def _split3(x):
    hi = x.astype(_BF16)
    r = x - hi.astype(_F32)
    mid = r.astype(_BF16)
    lo = (r - mid.astype(_F32)).astype(_BF16)
    return hi, mid, lo


def _inproj_kernel(x_ref, wu_ref, wq_ref, wk_ref, wvt_ref, wf_ref, bf_ref, pw_ref, ps_ref,
                   eq_ref, ek_ref,
                   pool_ref, qa_ref, ka_ref, vt_ref,
                   ubuf, sb1, sb2, sb3, carry):
    i = pl.program_id(1)
    tm = x_ref.shape[0]
    hl = POOL_HALO

    @pl.when(i == 0)
    def _():
        ubuf[0:hl, :] = jnp.zeros((hl, POOL_WIDTH), _F32)
        carry[...] = jnp.zeros(carry.shape, _F32)

    @pl.when(i > 0)
    def _():
        ubuf[0:hl, :] = ubuf[tm:tm + hl, :]

    xb = x_ref[...].astype(_BF16)

    lane = lax.broadcasted_iota(jnp.int32, (tm, LANES), 1)
    f = _dot(xb, wf_ref[...]) + bf_ref[...]
    lf = jnp.minimum(f, 0.0) - jnp.log1p(jnp.exp(-jnp.abs(f)))
    lf = jnp.where(lane < GATE_ONE_LANE, lf, 0.0)
    row = lax.broadcasted_iota(jnp.int32, (tm, tm), 0)
    col = lax.broadcasted_iota(jnp.int32, (tm, tm), 1)
    tri = jnp.where(col <= row, 1.0, 0.0).astype(_BF16)
    l_hi, l_mid, l_lo = _split3(lf)
    c = _dot(tri, l_hi) + _dot(tri, l_mid) + _dot(tri, l_lo) + carry[7:8, :]
    carry[...] = c[tm - 8:tm, :]
    c_hi, c_mid, c_lo = _split3(c)
    one = jnp.where(lane == GATE_ONE_LANE, 1.0, 0.0).astype(_BF16)
    gate = jnp.where(lane < 8, c_hi, jnp.where(lane < 16, c_mid, jnp.where(lane < 24, c_lo, one)))
    gq = _dot(gate, eq_ref[...]).astype(_BF16)
    gk = _dot(gate, ek_ref[...]).astype(_BF16)

    q = _dot(xb, wq_ref[...]).astype(_BF16)
    k = _dot(xb, wk_ref[...]).astype(_BF16)
    for p in range(N_PAIRS):
        src = slice(p * LANES, (p + 1) * LANES)
        qa_ref[:, p * PAIR_WIDTH:p * PAIR_WIDTH + LANES] = q[:, src]
        qa_ref[:, p * PAIR_WIDTH + LANES:(p + 1) * PAIR_WIDTH] = gq[:, src]
        ka_ref[:, p * PAIR_WIDTH:p * PAIR_WIDTH + LANES] = k[:, src]
        ka_ref[:, p * PAIR_WIDTH + LANES:(p + 1) * PAIR_WIDTH] = gk[:, src]

    vt = _dot_nt(wvt_ref[...], xb)
    pad_rows = VT_ROWS - HEAD_DIM
    ones_row = jnp.where(lax.broadcasted_iota(jnp.int32, (pad_rows, tm), 0) == 0, 1.0, 0.0)
    for h in range(N_HEADS):
        vt_ref[h, 0:HEAD_DIM, :] = vt[h * HEAD_DIM:(h + 1) * HEAD_DIM, :].astype(_BF16)
        vt_ref[h, HEAD_DIM:VT_ROWS, :] = ones_row.astype(_BF16)

    ubuf[hl:hl + tm, :] = _dot(xb, wu_ref[...])
    s2 = ubuf[8:tm + 32, :] + ubuf[7:tm + 31, :]
    sb1[8:tm + 32, :] = s2[:, 128:512]
    s4 = sb1[16:tm + 32, :] + sb1[14:tm + 30, :]
    sb2[16:tm + 32, :] = s4[:, 128:384]
    s8 = sb2[24:tm + 32, :] + sb2[20:tm + 28, :]
    sb3[24:tm + 32, :] = s8[:, 128:256]
    s16 = sb3[32:tm + 32, :] + sb3[24:tm + 24, :]
    sums = (s2[24:, 0:128], s4[16:, 0:128], s8[8:, 0:128], s16)
    pos = (i * tm + lax.broadcasted_iota(jnp.int32, (tm, 1), 0) + 1).astype(_F32)
    for g, win in enumerate(POOL_WINDOWS):
        sl = slice(g * POOL_GROUP_WIDTH, (g + 1) * POOL_GROUP_WIDTH)
        cnt = jnp.minimum(pos, float(win))
        d = sums[g] / cnt - ubuf[hl:hl + tm, sl]
        y = _dot(d.astype(_BF16), pw_ref[g]) * ps_ref[:, sl]
        pool_ref[:, sl] = y.astype(_BF16)


def _inproj(x, wu, wq, wk, wvt, wf, bf, pw, ps, eq, ek):
    b, s, d = x.shape
    tm = TM_IN
    const = lambda shape: pl.BlockSpec(shape, lambda bi, i: (0,) * len(shape))
    tile = lambda w: pl.BlockSpec((None, tm, w), lambda bi, i: (bi, i, 0))
    out_shape = [jax.ShapeDtypeStruct((b, s, POOL_WIDTH), _BF16),
                 jax.ShapeDtypeStruct((b, s, N_PAIRS * PAIR_WIDTH), _BF16),
                 jax.ShapeDtypeStruct((b, s, N_PAIRS * PAIR_WIDTH), _BF16),
                 jax.ShapeDtypeStruct((b, N_HEADS, s // tm, VT_ROWS, tm), _BF16)]
    out_specs = [tile(POOL_WIDTH), tile(N_PAIRS * PAIR_WIDTH), tile(N_PAIRS * PAIR_WIDTH),
                 pl.BlockSpec((None, N_HEADS, None, VT_ROWS, tm), lambda bi, i: (bi, 0, i, 0, 0))]
    return pl.pallas_call(
        _inproj_kernel,
        grid=(b, s // tm),
        in_specs=[tile(d), const(wu.shape), const(wq.shape), const(wk.shape), const(wvt.shape),
                  const(wf.shape), const(bf.shape), const(pw.shape), const(ps.shape),
                  const(eq.shape), const(ek.shape)],
        out_specs=out_specs,
        out_shape=out_shape,
        scratch_shapes=[pltpu.VMEM((tm + POOL_HALO, 512), _F32),
                        pltpu.VMEM((tm + POOL_HALO, 384), _F32),
                        pltpu.VMEM((tm + POOL_HALO, 256), _F32),
                        pltpu.VMEM((tm + POOL_HALO, 128), _F32),
                        pltpu.VMEM((8, LANES), _F32)],
        compiler_params=pltpu.CompilerParams(
            dimension_semantics=("arbitrary", "arbitrary"), vmem_limit_bytes=VMEM_LIMIT),
        name="inproj",
    )(x, wu, wq, wk, wvt, wf, bf, pw, ps, eq, ek)


def _attn_kernel(qa_ref, ka_ref, vt_ref, o_ref, qh_sc, s_ring, mx_ring, m_sc, acc_sc):
    i = pl.program_id(2)
    tq, tk, cw = T_Q, T_K, MXU_DIM
    g = GATE_COLS_PER_HEAD
    lane = lax.broadcasted_iota(jnp.int32, (tq, PAIR_WIDTH), 1)
    qf = qa_ref[...].astype(_F32)
    qh_sc[0] = jnp.where((lane < HEAD_DIM) | ((lane >= LANES) & (lane < LANES + g)),
                         qf, 0.0).astype(_BF16)
    qh_sc[1] = jnp.where(((lane >= HEAD_DIM) & (lane < LANES))
                         | ((lane >= LANES + g) & (lane < LANES + 2 * g)), qf, 0.0).astype(_BF16)
    m_sc[...] = jnp.full(m_sc.shape, NEG_INF, _F32)
    acc_sc[...] = jnp.zeros(acc_sc.shape, _F32)

    def produce(slot, blk, h, c, mask_off):
        kb = ka_ref[pl.ds(pl.multiple_of(blk * tk, tk), tk), :]
        s = _dot_nt(kb, qh_sc[h, c * cw:(c + 1) * cw, :])
        if mask_off is not None:
            kpos = lax.broadcasted_iota(jnp.int32, s.shape, 0) + mask_off
            qpos = lax.broadcasted_iota(jnp.int32, s.shape, 1) + c * cw
            s = jnp.where(kpos <= qpos, s, NEG_INF)
        s_ring[slot] = s
        mx_ring[slot] = jnp.max(s, axis=0, keepdims=True)

    def consume(slot, blk, h, c):
        cs = slice(c * cw, (c + 1) * cw)
        m_prev = m_sc[h, :, cs]
        m_new = jnp.maximum(m_prev, mx_ring[slot])
        alpha = jnp.exp(m_prev - m_new)
        p = jnp.exp(s_ring[slot] - m_new).astype(_BF16)
        acc_sc[h, :, cs] = alpha * acc_sc[h, :, cs] + _dot(vt_ref[h, blk], p)
        m_sc[h, :, cs] = m_new

    def run(entries, following):
        stream = entries + following[:ATT_LOOKAHEAD]
        for n, (blk, h, c, _) in enumerate(entries):
            produce((n + ATT_LOOKAHEAD) % ATT_RING, *stream[n + ATT_LOOKAHEAD])
            consume(n % ATT_RING, blk, h, c)

    nc = tq // cw
    per_q = tq // tk
    units = [(h, c) for h in range(2) for c in range(nc)]
    full_block = lambda blk: [(blk, h, c, None) for h, c in units]
    diag = [(per_q * i + 1, h, c, tk) for h, c in units if c >= nc // 2]
    diag += [(per_q * i, h, c, 0 if c < nc // 2 else None) for h, c in units]
    assert len(diag) % ATT_RING == 0 and len(units) % ATT_RING == 0

    for n in range(ATT_LOOKAHEAD):
        produce(n, *diag[n])
    run(diag, full_block(0))

    def body(j, carry):
        run(full_block(j), full_block(j + 1))
        return carry

    lax.fori_loop(0, per_q * i, body, 0)

    outs = []
    for h in range(2):
        a = acc_sc[h]
        outs.append(a[0:HEAD_DIM, :] / a[HEAD_DIM:HEAD_DIM + 1, :])
    o_ref[...] = jnp.concatenate(outs, axis=0).T.astype(o_ref.dtype)


def _attention(qa, ka, vt):
    b, s, _ = qa.shape
    tq, tk = T_Q, T_K
    return pl.pallas_call(
        _attn_kernel,
        grid=(b, N_PAIRS, s // tq),
        in_specs=[pl.BlockSpec((None, tq, PAIR_WIDTH), lambda bi, p, i: (bi, i, p)),
                  pl.BlockSpec((None, s, PAIR_WIDTH), lambda bi, p, i: (bi, 0, p)),
                  pl.BlockSpec((None, 2, s // tk, VT_ROWS, tk), lambda bi, p, i: (bi, p, 0, 0, 0))],
        out_specs=pl.BlockSpec((None, tq, LANES), lambda bi, p, i: (bi, i, p)),
        out_shape=jax.ShapeDtypeStruct((b, s, ATTN_WIDTH), _BF16),
        scratch_shapes=[pltpu.VMEM((2, tq, PAIR_WIDTH), _BF16),
                        pltpu.VMEM((ATT_RING, tk, MXU_DIM), _F32),
                        pltpu.VMEM((ATT_RING, 1, MXU_DIM), _F32),
                        pltpu.VMEM((2, 1, tq), _F32),
                        pltpu.VMEM((2, VT_ROWS, tq), _F32)],
        compiler_params=pltpu.CompilerParams(
            dimension_semantics=("arbitrary", "arbitrary", "arbitrary"),
            vmem_limit_bytes=VMEM_LIMIT),
        name="fox_attn",
    )(qa, ka, vt)


def _layer_norm(z, g, b):
    mu = jnp.mean(z, axis=-1, keepdims=True)
    zc = z - mu
    var = jnp.mean(zc * zc, axis=-1, keepdims=True)
    return zc * lax.rsqrt(var + LN_EPS) * g + b


def _post_kernel(x_ref, pool_ref, attn_ref, wop_ref, woa_ref, g1_ref, b1_ref,
                 w1_ref, w2_ref, g2_ref, b2_ref, o_ref):
    mix = _dot(pool_ref[...], wop_ref[...]) + _dot(attn_ref[...], woa_ref[...])
    x1 = _layer_norm(ALPHA * x_ref[...] + mix, g1_ref[...], b1_ref[...])
    x1b = x1.astype(_BF16)
    y = jnp.zeros_like(x1)
    for c in range(D_FF // FF_CHUNK):
        sl = slice(c * FF_CHUNK, (c + 1) * FF_CHUNK)
        hid = jnp.maximum(_dot(x1b, w1_ref[:, sl]), 0.0)
        y = y + _dot((hid * hid).astype(_BF16), w2_ref[sl, :])
    o_ref[...] = _layer_norm(ALPHA * x1 + y, g2_ref[...], b2_ref[...])


def _post(x, pool, attn, wop, woa, g1, b1, w1, w2, g2, b2):
    t, d = x.shape
    tm = TM_POST
    const = lambda a: pl.BlockSpec(a.shape, lambda i: (0,) * a.ndim, pipeline_mode=pl.Buffered(1))
    tile = lambda w: pl.BlockSpec((tm, w), lambda i: (i, 0))
    return pl.pallas_call(
        _post_kernel,
        grid=(t // tm,),
        in_specs=[tile(d), tile(512), tile(512), const(wop), const(woa), const(g1), const(b1),
                  const(w1), const(w2), const(g2), const(b2)],
        out_specs=tile(d),
        out_shape=jax.ShapeDtypeStruct((t, d), _F32),
        compiler_params=pltpu.CompilerParams(
            dimension_semantics=("arbitrary",), vmem_limit_bytes=VMEM_LIMIT),
        name="post",
    )(x, pool, attn, wop, woa, g1, b1, w1, w2, g2, b2)


def _gate_placement():
    eq = np.zeros((LANES, N_PAIRS * LANES), np.float32)
    ek = np.zeros((LANES, N_PAIRS * LANES), np.float32)
    for h in range(N_HEADS):
        base = (h // 2) * LANES + (h % 2) * GATE_COLS_PER_HEAD
        for piece in range(3):
            eq[8 * piece + h, base + piece] = 1.0
            eq[GATE_ONE_LANE, base + 3 + piece] = 1.0
            ek[GATE_ONE_LANE, base + piece] = 1.0
            ek[8 * piece + h, base + 3 + piece] = -1.0
    return jnp.asarray(eq, _BF16), jnp.asarray(ek, _BF16)


def kernel(x, w_in, b_f, pool_w, pool_scale, w_out, ln1_g, ln1_b, w_mlp1, w_mlp2, ln2_g, ln2_b):
    b, s, d = x.shape
    assert s % T_Q == 0 and (b * s) % TM_POST == 0
    eq, ek = _gate_placement()
    o_q = POOL_WIDTH
    o_k = o_q + ATTN_WIDTH
    o_v = o_k + ATTN_WIDTH
    o_f = o_v + ATTN_WIDTH
    row = lambda a: a.reshape(1, -1)
    for l in range(DEPTH):
        w = w_in[l]
        wu = w[:, :o_q].astype(_BF16)
        wq = (w[:, o_q:o_k] * HEAD_DIM ** -0.5).astype(_BF16)
        wk = w[:, o_k:o_v].astype(_BF16)
        wvt = w[:, o_v:o_f].T.astype(_BF16)
        wf = jnp.pad(jnp.tile(w[:, o_f:], (1, 3)), ((0, 0), (0, LANES - 3 * N_HEADS))).astype(_BF16)
        bf = jnp.pad(jnp.tile(b_f[l], 3), (0, LANES - 3 * N_HEADS)).reshape(1, LANES)
        pool, qa, ka, vt = _inproj(x, wu, wq, wk, wvt, wf, bf, pool_w[l].astype(_BF16),
                                   row(pool_scale[l]), eq, ek)
        attn = _attention(qa, ka, vt)
        wo = w_out[l].astype(_BF16)
        x = _post(x.reshape(b * s, d), pool.reshape(b * s, -1), attn.reshape(b * s, -1),
                  wo[:POOL_WIDTH], wo[POOL_WIDTH:], row(ln1_g[l]), row(ln1_b[l]),
                  w_mlp1[l].astype(_BF16), w_mlp2[l].astype(_BF16),
                  row(ln2_g[l]), row(ln2_b[l])).reshape(b, s, d)
    return x
```

```python
import jax
import jax.numpy as jnp
import numpy as np
from jax import lax
from jax.experimental import pallas as pl
from jax.experimental.pallas import tpu as pltpu

D_MODEL = 1024
DEPTH = 2
POOL_WIDTH = 512
POOL_WINDOWS = (2, 4, 8, 16)
POOL_GROUP_WIDTH = 128
ATTN_WIDTH = 512
HEAD_DIM = 64
N_HEADS = 8
N_PAIRS = N_HEADS // 2
D_FF = 4 * D_MODEL
LN_EPS = 1e-5
NEG_INF = -1e30
ALPHA = float((2.0 * DEPTH) ** 0.25)
LOG2E = 1.4426950408889634

LANES = 128
MXU_DIM = 256
POOL_HALO = 32
GATE_ONE_LANE = 24
GATE_COLS_PER_HEAD = 6
PAIR_WIDTH = 2 * LANES
VT_ROWS = 80

T_Q = 1024
T_K = 512
ATT_RING = 4
ATT_LOOKAHEAD = 3
TM_IN = T_K
TM_POST = 512
FF_CHUNK = 1024
VMEM_LIMIT = 56 * 1024 * 1024

_BF16 = jnp.bfloat16
_F32 = jnp.float32
_NT = (((1,), (1,)), ((), ()))


def _dot(a, b):
    return jnp.dot(a, b, preferred_element_type=_F32)


def _dot_nt(a, b):
    return lax.dot_general(a, b, _NT, preferred_element_type=_F32)


def _split3(x):
    hi = x.astype(_BF16)
    r = x - hi.astype(_F32)
    mid = r.astype(_BF16)
    lo = (r - mid.astype(_F32)).astype(_BF16)
    return hi, mid, lo


def _inproj_kernel(x_ref, wu_ref, wq_ref, wk_ref, wvt_ref, wf_ref, bf_ref, pw_ref, ps_ref,
                   eq_ref, ek_ref,
                   pool_ref, qa_ref, ka_ref, vt_ref,
                   ubuf, sb1, sb2, sb3, carry):
    i = pl.program_id(1)
    tm = x_ref.shape[0]
    hl = POOL_HALO

    @pl.when(i == 0)
    def _():
        ubuf[0:hl, :] = jnp.zeros((hl, POOL_WIDTH), _F32)
        carry[...] = jnp.zeros(carry.shape, _F32)

    @pl.when(i > 0)
    def _():
        ubuf[0:hl, :] = ubuf[tm:tm + hl, :]

    xb = x_ref[...].astype(_BF16)

    lane = lax.broadcasted_iota(jnp.int32, (tm, LANES), 1)
    f = _dot(xb, wf_ref[...]) + bf_ref[...]
    lf = (jnp.minimum(f, 0.0) - jnp.log1p(jnp.exp(-jnp.abs(f)))) * LOG2E
    lf = jnp.where(lane < GATE_ONE_LANE, lf, 0.0)
    row = lax.broadcasted_iota(jnp.int32, (tm, tm), 0)
    col = lax.broadcasted_iota(jnp.int32, (tm, tm), 1)
    tri = jnp.where(col <= row, 1.0, 0.0).astype(_BF16)
    l_hi, l_mid, l_lo = _split3(lf)
    c = _dot(tri, l_hi) + _dot(tri, l_mid) + _dot(tri, l_lo) + carry[7:8, :]
    carry[...] = c[tm - 8:tm, :]
    c_hi, c_mid, c_lo = _split3(c)
    one = jnp.where(lane == GATE_ONE_LANE, 1.0, 0.0).astype(_BF16)
    gate = jnp.where(lane < 8, c_hi, jnp.where(lane < 16, c_mid, jnp.where(lane < 24, c_lo, one)))
    gq = _dot(gate, eq_ref[...]).astype(_BF16)
    gk = _dot(gate, ek_ref[...]).astype(_BF16)

    q = _dot(xb, wq_ref[...]).astype(_BF16)
    k = _dot(xb, wk_ref[...]).astype(_BF16)
    for p in range(N_PAIRS):
        src = slice(p * LANES, (p + 1) * LANES)
        qa_ref[:, p * PAIR_WIDTH:p * PAIR_WIDTH + LANES] = q[:, src]
        qa_ref[:, p * PAIR_WIDTH + LANES:(p + 1) * PAIR_WIDTH] = gq[:, src]
        ka_ref[:, p * PAIR_WIDTH:p * PAIR_WIDTH + LANES] = k[:, src]
        ka_ref[:, p * PAIR_WIDTH + LANES:(p + 1) * PAIR_WIDTH] = gk[:, src]

    vt = _dot_nt(wvt_ref[...], xb)
    pad_rows = VT_ROWS - HEAD_DIM
    ones_row = jnp.where(lax.broadcasted_iota(jnp.int32, (pad_rows, tm), 0) == 0, 1.0, 0.0)
    for h in range(N_HEADS):
        vt_ref[h, 0:HEAD_DIM, :] = vt[h * HEAD_DIM:(h + 1) * HEAD_DIM, :].astype(_BF16)
        vt_ref[h, HEAD_DIM:VT_ROWS, :] = ones_row.astype(_BF16)

    ubuf[hl:hl + tm, :] = _dot(xb, wu_ref[...])
    s2 = ubuf[8:tm + 32, :] + ubuf[7:tm + 31, :]
    sb1[8:tm + 32, :] = s2[:, 128:512]
    s4 = sb1[16:tm + 32, :] + sb1[14:tm + 30, :]
    sb2[16:tm + 32, :] = s4[:, 128:384]
    s8 = sb2[24:tm + 32, :] + sb2[20:tm + 28, :]
    sb3[24:tm + 32, :] = s8[:, 128:256]
    s16 = sb3[32:tm + 32, :] + sb3[24:tm + 24, :]
    sums = (s2[24:, 0:128], s4[16:, 0:128], s8[8:, 0:128], s16)
    pos = (i * tm + lax.broadcasted_iota(jnp.int32, (tm, 1), 0) + 1).astype(_F32)
    for g, win in enumerate(POOL_WINDOWS):
        sl = slice(g * POOL_GROUP_WIDTH, (g + 1) * POOL_GROUP_WIDTH)
        cnt = jnp.minimum(pos, float(win))
        d = sums[g] / cnt - ubuf[hl:hl + tm, sl]
        y = _dot(d.astype(_BF16), pw_ref[g]) * ps_ref[:, sl]
        pool_ref[:, sl] = y.astype(_BF16)


def _inproj(x, wu, wq, wk, wvt, wf, bf, pw, ps, eq, ek):
    b, s, d = x.shape
    tm = TM_IN
    const = lambda shape: pl.BlockSpec(shape, lambda bi, i: (0,) * len(shape))
    tile = lambda w: pl.BlockSpec((None, tm, w), lambda bi, i: (bi, i, 0))
    out_shape = [jax.ShapeDtypeStruct((b, s, POOL_WIDTH), _BF16),
                 jax.ShapeDtypeStruct((b, s, N_PAIRS * PAIR_WIDTH), _BF16),
                 jax.ShapeDtypeStruct((b, s, N_PAIRS * PAIR_WIDTH), _BF16),
                 jax.ShapeDtypeStruct((b, N_HEADS, s // tm, VT_ROWS, tm), _BF16)]
    out_specs = [tile(POOL_WIDTH), tile(N_PAIRS * PAIR_WIDTH), tile(N_PAIRS * PAIR_WIDTH),
                 pl.BlockSpec((None, N_HEADS, None, VT_ROWS, tm), lambda bi, i: (bi, 0, i, 0, 0))]
    return pl.pallas_call(
        _inproj_kernel,
        grid=(b, s // tm),
        in_specs=[tile(d), const(wu.shape), const(wq.shape), const(wk.shape), const(wvt.shape),
                  const(wf.shape), const(bf.shape), const(pw.shape), const(ps.shape),
                  const(eq.shape), const(ek.shape)],
        out_specs=out_specs,
        out_shape=out_shape,
        scratch_shapes=[pltpu.VMEM((tm + POOL_HALO, 512), _F32),
                        pltpu.VMEM((tm + POOL_HALO, 384), _F32),
                        pltpu.VMEM((tm + POOL_HALO, 256), _F32),
                        pltpu.VMEM((tm + POOL_HALO, 128), _F32),
                        pltpu.VMEM((8, LANES), _F32)],
        compiler_params=pltpu.CompilerParams(
            dimension_semantics=("arbitrary", "arbitrary"), vmem_limit_bytes=VMEM_LIMIT),
        name="inproj",
    )(x, wu, wq, wk, wvt, wf, bf, pw, ps, eq, ek)


def _attn_kernel(qa_ref, ka_ref, vt_ref, o_ref, qh_sc, m_sc, acc_sc, *rings):
    s_ring, mx_ring = rings[:ATT_RING], rings[ATT_RING:]
    i = pl.program_id(2)
    tq, tk, cw = T_Q, T_K, MXU_DIM
    g = GATE_COLS_PER_HEAD
    lane = lax.broadcasted_iota(jnp.int32, (tq, PAIR_WIDTH), 1)
    qf = qa_ref[...].astype(_F32)
    qh_sc[0] = jnp.where((lane < HEAD_DIM) | ((lane >= LANES) & (lane < LANES + g)),
                         qf, 0.0).astype(_BF16)
    qh_sc[1] = jnp.where(((lane >= HEAD_DIM) & (lane < LANES))
                         | ((lane >= LANES + g) & (lane < LANES + 2 * g)), qf, 0.0).astype(_BF16)
    m_sc[...] = jnp.full(m_sc.shape, NEG_INF, _F32)
    acc_sc[...] = jnp.zeros(acc_sc.shape, _F32)

    def produce(slot, blk, h, c, mask_off):
        kb = ka_ref[pl.ds(pl.multiple_of(blk * tk, tk), tk), :]
        s = _dot_nt(kb, qh_sc[h, c * cw:(c + 1) * cw, :])
        if mask_off is not None:
            kpos = lax.broadcasted_iota(jnp.int32, s.shape, 0) + mask_off
            qpos = lax.broadcasted_iota(jnp.int32, s.shape, 1) + c * cw
            s = jnp.where(kpos <= qpos, s, NEG_INF)
        s_ring[slot][...] = s
        mx_ring[slot][...] = jnp.max(s, axis=0, keepdims=True)

    def consume(slot, blk, h, c):
        cs = slice(c * cw, (c + 1) * cw)
        m_prev = m_sc[h, :, cs]
        m_new = jnp.maximum(m_prev, mx_ring[slot][...])
        alpha = jnp.exp2(m_prev - m_new)
        p = jnp.exp2(s_ring[slot][...] - m_new).astype(_BF16)
        acc_sc[h, :, cs] = alpha * acc_sc[h, :, cs] + _dot(vt_ref[h, blk], p)
        m_sc[h, :, cs] = m_new

    def run(entries, following, pos0):
        stream = entries + following[:ATT_LOOKAHEAD]
        for n, (blk, h, c, _) in enumerate(entries):
            produce((pos0 + n + ATT_LOOKAHEAD) % ATT_RING, *stream[n + ATT_LOOKAHEAD])
            consume((pos0 + n) % ATT_RING, blk, h, c)

    nc = tq // cw
    per_q = tq // tk
    units = [(h, c) for h in range(2) for c in range(nc)]
    full_block = lambda blk: [(blk, h, c, None) for h, c in units]
    diag = [(per_q * i + 1, h, c, tk) for h, c in units if c >= nc // 2]
    diag += [(per_q * i, h, c, 0 if c < nc // 2 else None) for h, c in units]
    assert len(units) % ATT_RING == 0 and ATT_LOOKAHEAD < ATT_RING

    for n in range(ATT_LOOKAHEAD):
        produce(n, *diag[n])
    run(diag, full_block(0), 0)

    def body(j, carry):
        blocks = [per_q * j + r for r in range(per_q)]
        run(sum((full_block(blk) for blk in blocks), []), full_block(per_q * (j + 1)), len(diag))
        return carry

    lax.fori_loop(0, i, body, 0)

    outs = []
    for h in range(2):
        a = acc_sc[h]
        outs.append(a[0:HEAD_DIM, :] / a[HEAD_DIM:HEAD_DIM + 1, :])
    o_ref[...] = jnp.concatenate(outs, axis=0).T.astype(o_ref.dtype)


def _attention(qa, ka, vt):
    b, s, _ = qa.shape
    tq, tk = T_Q, T_K
    return pl.pallas_call(
        _attn_kernel,
        grid=(b, N_PAIRS, s // tq),
        in_specs=[pl.BlockSpec((None, tq, PAIR_WIDTH), lambda bi, p, i: (bi, i, p)),
                  pl.BlockSpec((None, s, PAIR_WIDTH), lambda bi, p, i: (bi, 0, p)),
                  pl.BlockSpec((None, 2, s // tk, VT_ROWS, tk), lambda bi, p, i: (bi, p, 0, 0, 0))],
        out_specs=pl.BlockSpec((None, tq, LANES), lambda bi, p, i: (bi, i, p)),
        out_shape=jax.ShapeDtypeStruct((b, s, ATTN_WIDTH), _BF16),
        scratch_shapes=([pltpu.VMEM((2, tq, PAIR_WIDTH), _BF16),
                         pltpu.VMEM((2, 1, tq), _F32),
                         pltpu.VMEM((2, VT_ROWS, tq), _F32)]
                        + [pltpu.VMEM((tk, MXU_DIM), _F32) for _ in range(ATT_RING)]
                        + [pltpu.VMEM((1, MXU_DIM), _F32) for _ in range(ATT_RING)]),
        compiler_params=pltpu.CompilerParams(
            dimension_semantics=("arbitrary", "arbitrary", "arbitrary"),
            vmem_limit_bytes=VMEM_LIMIT),
        name="fox_attn",
    )(qa, ka, vt)


def _layer_norm(z, g, b):
    mu = jnp.mean(z, axis=-1, keepdims=True)
    zc = z - mu
    var = jnp.mean(zc * zc, axis=-1, keepdims=True)
    return zc * lax.rsqrt(var + LN_EPS) * g + b


def _post_kernel(x_ref, pool_ref, attn_ref, wop_ref, woa_ref, g1_ref, b1_ref,
                 w1_ref, w2_ref, g2_ref, b2_ref, o_ref):
    mix = _dot(pool_ref[...], wop_ref[...]) + _dot(attn_ref[...], woa_ref[...])
    x1 = _layer_norm(ALPHA * x_ref[...] + mix, g1_ref[...], b1_ref[...])
    x1b = x1.astype(_BF16)
    y = jnp.zeros_like(x1)
    for c in range(D_FF // FF_CHUNK):
        sl = slice(c * FF_CHUNK, (c + 1) * FF_CHUNK)
        hid = jnp.maximum(_dot(x1b, w1_ref[:, sl]), 0.0)
        y = y + _dot((hid * hid).astype(_BF16), w2_ref[sl, :])
    o_ref[...] = _layer_norm(ALPHA * x1 + y, g2_ref[...], b2_ref[...])


def _post(x, pool, attn, wop, woa, g1, b1, w1, w2, g2, b2):
    t, d = x.shape
    tm = TM_POST
    const = lambda a: pl.BlockSpec(a.shape, lambda i: (0,) * a.ndim, pipeline_mode=pl.Buffered(1))
    tile = lambda w: pl.BlockSpec((tm, w), lambda i: (i, 0))
    return pl.pallas_call(
        _post_kernel,
        grid=(t // tm,),
        in_specs=[tile(d), tile(512), tile(512), const(wop), const(woa), const(g1), const(b1),
                  const(w1), const(w2), const(g2), const(b2)],
        out_specs=tile(d),
        out_shape=jax.ShapeDtypeStruct((t, d), _F32),
        compiler_params=pltpu.CompilerParams(
            dimension_semantics=("arbitrary",), vmem_limit_bytes=VMEM_LIMIT),
        name="post",
    )(x, pool, attn, wop, woa, g1, b1, w1, w2, g2, b2)


def _gate_placement():
    eq = np.zeros((LANES, N_PAIRS * LANES), np.float32)
    ek = np.zeros((LANES, N_PAIRS * LANES), np.float32)
    for h in range(N_HEADS):
        base = (h // 2) * LANES + (h % 2) * GATE_COLS_PER_HEAD
        for piece in range(3):
            eq[8 * piece + h, base + piece] = 1.0
            eq[GATE_ONE_LANE, base + 3 + piece] = 1.0
            ek[GATE_ONE_LANE, base + piece] = 1.0
            ek[8 * piece + h, base + 3 + piece] = -1.0
    return jnp.asarray(eq, _BF16), jnp.asarray(ek, _BF16)


def kernel(x, w_in, b_f, pool_w, pool_scale, w_out, ln1_g, ln1_b, w_mlp1, w_mlp2, ln2_g, ln2_b):
    b, s, d = x.shape
    assert s % T_Q == 0 and (b * s) % TM_POST == 0
    eq, ek = _gate_placement()
    o_q = POOL_WIDTH
    o_k = o_q + ATTN_WIDTH
    o_v = o_k + ATTN_WIDTH
    o_f = o_v + ATTN_WIDTH
    row = lambda a: a.reshape(1, -1)
    for l in range(DEPTH):
        w = w_in[l]
        wu = w[:, :o_q].astype(_BF16)
        wq = (w[:, o_q:o_k] * (HEAD_DIM ** -0.5 * LOG2E)).astype(_BF16)
        wk = w[:, o_k:o_v].astype(_BF16)
        wvt = w[:, o_v:o_f].T.astype(_BF16)
        wf = jnp.pad(jnp.tile(w[:, o_f:], (1, 3)), ((0, 0), (0, LANES - 3 * N_HEADS))).astype(_BF16)
        bf = jnp.pad(jnp.tile(b_f[l], 3), (0, LANES - 3 * N_HEADS)).reshape(1, LANES)
        pool, qa, ka, vt = _inproj(x, wu, wq, wk, wvt, wf, bf, pool_w[l].astype(_BF16),
                                   row(pool_scale[l]), eq, ek)
        attn = _attention(qa, ka, vt)
        wo = w_out[l].astype(_BF16)
        x = _post(x.reshape(b * s, d), pool.reshape(b * s, -1), attn.reshape(b * s, -1),
                  wo[:POOL_WIDTH], wo[POOL_WIDTH:], row(ln1_g[l]), row(ln1_b[l]),
                  w_mlp1[l].astype(_BF16), w_mlp2[l].astype(_BF16),
                  row(ln2_g[l]), row(ln2_b[l])).reshape(b, s, d)
    return x
```

```python
import jax
import jax.numpy as jnp
import numpy as np
from jax import lax
from jax.experimental import pallas as pl
from jax.experimental.pallas import tpu as pltpu

D_MODEL = 1024
DEPTH = 2
POOL_WIDTH = 512
POOL_WINDOWS = (2, 4, 8, 16)
POOL_GROUP_WIDTH = 128
ATTN_WIDTH = 512
HEAD_DIM = 64
N_HEADS = 8
N_PAIRS = N_HEADS // 2
D_FF = 4 * D_MODEL
LN_EPS = 1e-5
NEG_INF = -1e30
ALPHA = float((2.0 * DEPTH) ** 0.25)
LOG2E = 1.4426950408889634

LANES = 128
MXU_DIM = 256
POOL_HALO = 32
GATE_ONE_LANE = 24
GATE_COLS_PER_HEAD = 6
PAIR_WIDTH = 2 * LANES
VT_ROWS = 80

T_Q = 2048
T_K = 512
ATT_BLOCKS_PER_ITER = 2
ATT_RING = 4
ATT_LOOKAHEAD = 3
TM_IN = T_K
TM_POST = 512
FF_CHUNK = 1024
VMEM_LIMIT = 56 * 1024 * 1024

_BF16 = jnp.bfloat16
_F32 = jnp.float32
_NT = (((1,), (1,)), ((), ()))


def _dot(a, b):
    return jnp.dot(a, b, preferred_element_type=_F32)


def _dot_nt(a, b):
    return lax.dot_general(a, b, _NT, preferred_element_type=_F32)


def _split3(x):
    hi = x.astype(_BF16)
    r = x - hi.astype(_F32)
    mid = r.astype(_BF16)
    lo = (r - mid.astype(_F32)).astype(_BF16)
    return hi, mid, lo


def _inproj_kernel(x_ref, wu_ref, wq_ref, wk_ref, wvt_ref, wf_ref, bf_ref, pw_ref, ps_ref,
                   eq_ref, ek_ref,
                   pool_ref, qa_ref, ka_ref, vt_ref,
                   ubuf, sb1, sb2, sb3, carry):
    i = pl.program_id(1)
    tm = x_ref.shape[0]
    hl = POOL_HALO

    @pl.when(i == 0)
    def _():
        ubuf[0:hl, :] = jnp.zeros((hl, POOL_WIDTH), _F32)
        carry[...] = jnp.zeros(carry.shape, _F32)

    @pl.when(i > 0)
    def _():
        ubuf[0:hl, :] = ubuf[tm:tm + hl, :]

    xb = x_ref[...].astype(_BF16)

    lane = lax.broadcasted_iota(jnp.int32, (tm, LANES), 1)
    f = _dot(xb, wf_ref[...]) + bf_ref[...]
    lf = (jnp.minimum(f, 0.0) - jnp.log1p(jnp.exp(-jnp.abs(f)))) * LOG2E
    lf = jnp.where(lane < GATE_ONE_LANE, lf, 0.0)
    row = lax.broadcasted_iota(jnp.int32, (tm, tm), 0)
    col = lax.broadcasted_iota(jnp.int32, (tm, tm), 1)
    tri = jnp.where(col <= row, 1.0, 0.0).astype(_BF16)
    l_hi, l_mid, l_lo = _split3(lf)
    c = _dot(tri, l_hi) + _dot(tri, l_mid) + _dot(tri, l_lo) + carry[7:8, :]
    carry[...] = c[tm - 8:tm, :]
    c_hi, c_mid, c_lo = _split3(c)
    one = jnp.where(lane == GATE_ONE_LANE, 1.0, 0.0).astype(_BF16)
    gate = jnp.where(lane < 8, c_hi, jnp.where(lane < 16, c_mid, jnp.where(lane < 24, c_lo, one)))
    gq = _dot(gate, eq_ref[...]).astype(_BF16)
    gk = _dot(gate, ek_ref[...]).astype(_BF16)

    q = _dot(xb, wq_ref[...]).astype(_BF16)
    k = _dot(xb, wk_ref[...]).astype(_BF16)
    for p in range(N_PAIRS):
        src = slice(p * LANES, (p + 1) * LANES)
        qa_ref[:, p * PAIR_WIDTH:p * PAIR_WIDTH + LANES] = q[:, src]
        qa_ref[:, p * PAIR_WIDTH + LANES:(p + 1) * PAIR_WIDTH] = gq[:, src]
        ka_ref[:, p * PAIR_WIDTH:p * PAIR_WIDTH + LANES] = k[:, src]
        ka_ref[:, p * PAIR_WIDTH + LANES:(p + 1) * PAIR_WIDTH] = gk[:, src]

    vt = _dot_nt(wvt_ref[...], xb)
    pad_rows = VT_ROWS - HEAD_DIM
    ones_row = jnp.where(lax.broadcasted_iota(jnp.int32, (pad_rows, tm), 0) == 0, 1.0, 0.0)
    for h in range(N_HEADS):
        vt_ref[h, 0:HEAD_DIM, :] = vt[h * HEAD_DIM:(h + 1) * HEAD_DIM, :].astype(_BF16)
        vt_ref[h, HEAD_DIM:VT_ROWS, :] = ones_row.astype(_BF16)

    ubuf[hl:hl + tm, :] = _dot(xb, wu_ref[...])
    s2 = ubuf[8:tm + 32, :] + ubuf[7:tm + 31, :]
    sb1[8:tm + 32, :] = s2[:, 128:512]
    s4 = sb1[16:tm + 32, :] + sb1[14:tm + 30, :]
    sb2[16:tm + 32, :] = s4[:, 128:384]
    s8 = sb2[24:tm + 32, :] + sb2[20:tm + 28, :]
    sb3[24:tm + 32, :] = s8[:, 128:256]
    s16 = sb3[32:tm + 32, :] + sb3[24:tm + 24, :]
    sums = (s2[24:, 0:128], s4[16:, 0:128], s8[8:, 0:128], s16)
    pos = (i * tm + lax.broadcasted_iota(jnp.int32, (tm, 1), 0) + 1).astype(_F32)
    for g, win in enumerate(POOL_WINDOWS):
        sl = slice(g * POOL_GROUP_WIDTH, (g + 1) * POOL_GROUP_WIDTH)
        cnt = jnp.minimum(pos, float(win))
        d = sums[g] / cnt - ubuf[hl:hl + tm, sl]
        y = _dot(d.astype(_BF16), pw_ref[g]) * ps_ref[:, sl]
        pool_ref[:, sl] = y.astype(_BF16)


def _inproj(x, wu, wq, wk, wvt, wf, bf, pw, ps, eq, ek):
    b, s, d = x.shape
    tm = TM_IN
    const = lambda shape: pl.BlockSpec(shape, lambda bi, i: (0,) * len(shape))
    tile = lambda w: pl.BlockSpec((None, tm, w), lambda bi, i: (bi, i, 0))
    out_shape = [jax.ShapeDtypeStruct((b, s, POOL_WIDTH), _BF16),
                 jax.ShapeDtypeStruct((b, s, N_PAIRS * PAIR_WIDTH), _BF16),
                 jax.ShapeDtypeStruct((b, s, N_PAIRS * PAIR_WIDTH), _BF16),
                 jax.ShapeDtypeStruct((b, N_HEADS, s // tm, VT_ROWS, tm), _BF16)]
    out_specs = [tile(POOL_WIDTH), tile(N_PAIRS * PAIR_WIDTH), tile(N_PAIRS * PAIR_WIDTH),
                 pl.BlockSpec((None, N_HEADS, None, VT_ROWS, tm), lambda bi, i: (bi, 0, i, 0, 0))]
    return pl.pallas_call(
        _inproj_kernel,
        grid=(b, s // tm),
        in_specs=[tile(d), const(wu.shape), const(wq.shape), const(wk.shape), const(wvt.shape),
                  const(wf.shape), const(bf.shape), const(pw.shape), const(ps.shape),
                  const(eq.shape), const(ek.shape)],
        out_specs=out_specs,
        out_shape=out_shape,
        scratch_shapes=[pltpu.VMEM((tm + POOL_HALO, 512), _F32),
                        pltpu.VMEM((tm + POOL_HALO, 384), _F32),
                        pltpu.VMEM((tm + POOL_HALO, 256), _F32),
                        pltpu.VMEM((tm + POOL_HALO, 128), _F32),
                        pltpu.VMEM((8, LANES), _F32)],
        compiler_params=pltpu.CompilerParams(
            dimension_semantics=("arbitrary", "arbitrary"), vmem_limit_bytes=VMEM_LIMIT),
        name="inproj",
    )(x, wu, wq, wk, wvt, wf, bf, pw, ps, eq, ek)


def _attn_kernel(qa_ref, ka_ref, vt_ref, o_ref, qh_sc, m_sc, acc_sc, *rings):
    s_ring, mx_ring = rings[:ATT_RING], rings[ATT_RING:]
    i = pl.program_id(2)
    tq, tk, cw = T_Q, T_K, MXU_DIM
    g = GATE_COLS_PER_HEAD
    lane = lax.broadcasted_iota(jnp.int32, (tq, PAIR_WIDTH), 1)
    qf = qa_ref[...].astype(_F32)
    qh_sc[0] = jnp.where((lane < HEAD_DIM) | ((lane >= LANES) & (lane < LANES + g)),
                         qf, 0.0).astype(_BF16)
    qh_sc[1] = jnp.where(((lane >= HEAD_DIM) & (lane < LANES))
                         | ((lane >= LANES + g) & (lane < LANES + 2 * g)), qf, 0.0).astype(_BF16)
    m_sc[...] = jnp.full(m_sc.shape, NEG_INF, _F32)
    acc_sc[...] = jnp.zeros(acc_sc.shape, _F32)

    def produce(slot, blk, h, c, mask_off):
        kb = ka_ref[pl.ds(pl.multiple_of(blk * tk, tk), tk), :]
        s = _dot_nt(kb, qh_sc[h, c * cw:(c + 1) * cw, :])
        if mask_off is not None:
            kpos = lax.broadcasted_iota(jnp.int32, s.shape, 0) + mask_off
            qpos = lax.broadcasted_iota(jnp.int32, s.shape, 1) + c * cw
            s = jnp.where(kpos <= qpos, s, NEG_INF)
        s_ring[slot][...] = s
        mx_ring[slot][...] = jnp.max(s, axis=0, keepdims=True)

    def consume(slot, blk, h, c):
        cs = slice(c * cw, (c + 1) * cw)
        m_prev = m_sc[h, :, cs]
        m_new = jnp.maximum(m_prev, mx_ring[slot][...])
        alpha = jnp.exp2(m_prev - m_new)
        p = jnp.exp2(s_ring[slot][...] - m_new).astype(_BF16)
        acc_sc[h, :, cs] = alpha * acc_sc[h, :, cs] + _dot(vt_ref[h, blk], p)
        m_sc[h, :, cs] = m_new

    def run(entries, following, pos0):
        stream = entries + following[:ATT_LOOKAHEAD]
        for n, (blk, h, c, _) in enumerate(entries):
            produce((pos0 + n + ATT_LOOKAHEAD) % ATT_RING, *stream[n + ATT_LOOKAHEAD])
            consume((pos0 + n) % ATT_RING, blk, h, c)

    nc = tq // cw
    per_q = tq // tk
    units = [(h, c) for h in range(2) for c in range(nc)]
    full_block = lambda blk: [(blk, h, c, None) for h, c in units]
    cpb = tk // cw
    diag = []
    for r in reversed(range(per_q)):
        diag += [(per_q * i + r, h, c, r * tk if c < (r + 1) * cpb else None)
                 for h, c in units if c >= r * cpb]
    assert len(units) % ATT_RING == 0 and ATT_LOOKAHEAD < ATT_RING
    assert per_q % ATT_BLOCKS_PER_ITER == 0

    for n in range(ATT_LOOKAHEAD):
        produce(n, *diag[n])
    run(diag, full_block(0), 0)

    def body(j, carry):
        blocks = [ATT_BLOCKS_PER_ITER * j + r for r in range(ATT_BLOCKS_PER_ITER)]
        run(sum((full_block(blk) for blk in blocks), []),
            full_block(ATT_BLOCKS_PER_ITER * (j + 1)), len(diag))
        return carry

    lax.fori_loop(0, (per_q // ATT_BLOCKS_PER_ITER) * i, body, 0)

    outs = []
    for h in range(2):
        a = acc_sc[h]
        outs.append(a[0:HEAD_DIM, :] / a[HEAD_DIM:HEAD_DIM + 1, :])
    o_ref[...] = jnp.concatenate(outs, axis=0).T.astype(o_ref.dtype)


def _attention(qa, ka, vt):
    b, s, _ = qa.shape
    tq, tk = T_Q, T_K
    return pl.pallas_call(
        _attn_kernel,
        grid=(b, N_PAIRS, s // tq),
        in_specs=[pl.BlockSpec((None, tq, PAIR_WIDTH), lambda bi, p, i: (bi, i, p)),
                  pl.BlockSpec((None, s, PAIR_WIDTH), lambda bi, p, i: (bi, 0, p)),
                  pl.BlockSpec((None, 2, s // tk, VT_ROWS, tk), lambda bi, p, i: (bi, p, 0, 0, 0))],
        out_specs=pl.BlockSpec((None, tq, LANES), lambda bi, p, i: (bi, i, p)),
        out_shape=jax.ShapeDtypeStruct((b, s, ATTN_WIDTH), _BF16),
        scratch_shapes=([pltpu.VMEM((2, tq, PAIR_WIDTH), _BF16),
                         pltpu.VMEM((2, 1, tq), _F32),
                         pltpu.VMEM((2, VT_ROWS, tq), _F32)]
                        + [pltpu.VMEM((tk, MXU_DIM), _F32) for _ in range(ATT_RING)]
                        + [pltpu.VMEM((1, MXU_DIM), _F32) for _ in range(ATT_RING)]),
        compiler_params=pltpu.CompilerParams(
            dimension_semantics=("arbitrary", "arbitrary", "arbitrary"),
            vmem_limit_bytes=VMEM_LIMIT),
        name="fox_attn",
    )(qa, ka, vt)


def _layer_norm(z, g, b):
    mu = jnp.mean(z, axis=-1, keepdims=True)
    zc = z - mu
    var = jnp.mean(zc * zc, axis=-1, keepdims=True)
    return zc * lax.rsqrt(var + LN_EPS) * g + b


def _post_kernel(x_ref, pool_ref, attn_ref, wop_ref, woa_ref, g1_ref, b1_ref,
                 w1_ref, w2_ref, g2_ref, b2_ref, o_ref):
    mix = _dot(pool_ref[...], wop_ref[...]) + _dot(attn_ref[...], woa_ref[...])
    x1 = _layer_norm(ALPHA * x_ref[...] + mix, g1_ref[...], b1_ref[...])
    x1b = x1.astype(_BF16)
    y = jnp.zeros_like(x1)
    for c in range(D_FF // FF_CHUNK):
        sl = slice(c * FF_CHUNK, (c + 1) * FF_CHUNK)
        hid = jnp.maximum(_dot(x1b, w1_ref[:, sl]), 0.0)
        y = y + _dot((hid * hid).astype(_BF16), w2_ref[sl, :])
    o_ref[...] = _layer_norm(ALPHA * x1 + y, g2_ref[...], b2_ref[...])


def _post(x, pool, attn, wop, woa, g1, b1, w1, w2, g2, b2):
    t, d = x.shape
    tm = TM_POST
    const = lambda a: pl.BlockSpec(a.shape, lambda i: (0,) * a.ndim, pipeline_mode=pl.Buffered(1))
    tile = lambda w: pl.BlockSpec((tm, w), lambda i: (i, 0))
    return pl.pallas_call(
        _post_kernel,
        grid=(t // tm,),
        in_specs=[tile(d), tile(512), tile(512), const(wop), const(woa), const(g1), const(b1),
                  const(w1), const(w2), const(g2), const(b2)],
        out_specs=tile(d),
        out_shape=jax.ShapeDtypeStruct((t, d), _F32),
        compiler_params=pltpu.CompilerParams(
            dimension_semantics=("arbitrary",), vmem_limit_bytes=VMEM_LIMIT),
        name="post",
    )(x, pool, attn, wop, woa, g1, b1, w1, w2, g2, b2)


def _gate_placement():
    eq = np.zeros((LANES, N_PAIRS * LANES), np.float32)
    ek = np.zeros((LANES, N_PAIRS * LANES), np.float32)
    for h in range(N_HEADS):
        base = (h // 2) * LANES + (h % 2) * GATE_COLS_PER_HEAD
        for piece in range(3):
            eq[8 * piece + h, base + piece] = 1.0
            eq[GATE_ONE_LANE, base + 3 + piece] = 1.0
            ek[GATE_ONE_LANE, base + piece] = 1.0
            ek[8 * piece + h, base + 3 + piece] = -1.0
    return jnp.asarray(eq, _BF16), jnp.asarray(ek, _BF16)


def kernel(x, w_in, b_f, pool_w, pool_scale, w_out, ln1_g, ln1_b, w_mlp1, w_mlp2, ln2_g, ln2_b):
    b, s, d = x.shape
    assert s % T_Q == 0 and (b * s) % TM_POST == 0
    eq, ek = _gate_placement()
    o_q = POOL_WIDTH
    o_k = o_q + ATTN_WIDTH
    o_v = o_k + ATTN_WIDTH
    o_f = o_v + ATTN_WIDTH
    row = lambda a: a.reshape(1, -1)
    for l in range(DEPTH):
        w = w_in[l]
        wu = w[:, :o_q].astype(_BF16)
        wq = (w[:, o_q:o_k] * (HEAD_DIM ** -0.5 * LOG2E)).astype(_BF16)
        wk = w[:, o_k:o_v].astype(_BF16)
        wvt = w[:, o_v:o_f].T.astype(_BF16)
        wf = jnp.pad(jnp.tile(w[:, o_f:], (1, 3)), ((0, 0), (0, LANES - 3 * N_HEADS))).astype(_BF16)
        bf = jnp.pad(jnp.tile(b_f[l], 3), (0, LANES - 3 * N_HEADS)).reshape(1, LANES)
        pool, qa, ka, vt = _inproj(x, wu, wq, wk, wvt, wf, bf, pool_w[l].astype(_BF16),
                                   row(pool_scale[l]), eq, ek)
        attn = _attention(qa, ka, vt)
        wo = w_out[l].astype(_BF16)
        x = _post(x.reshape(b * s, d), pool.reshape(b * s, -1), attn.reshape(b * s, -1),
                  wo[:POOL_WIDTH], wo[POOL_WIDTH:], row(ln1_g[l]), row(ln1_b[l]),
                  w_mlp1[l].astype(_BF16), w_mlp2[l].astype(_BF16),
                  row(ln2_g[l]), row(ln2_b[l])).reshape(b, s, d)
    return x
```

```python
import jax
import jax.numpy as jnp
import numpy as np
from jax import lax
from jax.experimental import pallas as pl
from jax.experimental.pallas import tpu as pltpu

D_MODEL = 1024
DEPTH = 2
POOL_WIDTH = 512
POOL_WINDOWS = (2, 4, 8, 16)
POOL_GROUP_WIDTH = 128
ATTN_WIDTH = 512
HEAD_DIM = 64
N_HEADS = 8
N_PAIRS = N_HEADS // 2
D_FF = 4 * D_MODEL
LN_EPS = 1e-5
NEG_INF = -1e30
ALPHA = float((2.0 * DEPTH) ** 0.25)
LOG2E = 1.4426950408889634

LANES = 128
MXU_DIM = 256
POOL_HALO = 32
GATE_ONE_LANE = 24
GATE_COPIES = 9
GATE_COLS_PER_HEAD = 6
PAIR_WIDTH = 2 * LANES
VT_ROWS = 80

T_Q = 2048
T_K = 512
ATT_BLOCKS_PER_ITER = 2
ATT_RING = 4
ATT_LOOKAHEAD = 3
TM_IN = T_K
TM_POST = 1024
POST_SUBTILES = 4
FF_CHUNK = 1024
VMEM_LIMIT = 56 * 1024 * 1024

_BF16 = jnp.bfloat16
_F32 = jnp.float32
_NT = (((1,), (1,)), ((), ()))


def _dot(a, b):
    return jnp.dot(a, b, preferred_element_type=_F32)


def _dot_nt(a, b):
    return lax.dot_general(a, b, _NT, preferred_element_type=_F32)


def _split3(x):
    hi = x.astype(_BF16)
    r = x - hi.astype(_F32)
    mid = r.astype(_BF16)
    lo = (r - mid.astype(_F32)).astype(_BF16)
    return hi, mid, lo


def _inproj_kernel(x_ref, wu_ref, wq_ref, wk_ref, wvt_ref, wf_ref, bf_ref, pw_ref, ps_ref,
                   eq_ref, ek_ref,
                   pool_ref, qa_ref, ka_ref, vt_ref,
                   ubuf, sb1, sb2, sb3, carry):
    i = pl.program_id(1)
    tm = x_ref.shape[0]
    hl = POOL_HALO

    @pl.when(i == 0)
    def _():
        ubuf[0:hl, :] = jnp.zeros((hl, POOL_WIDTH), _F32)
        carry[...] = jnp.zeros(carry.shape, _F32)

    @pl.when(i > 0)
    def _():
        ubuf[0:hl, :] = ubuf[tm:tm + hl, :]

    xb = x_ref[...].astype(_BF16)

    lane = lax.broadcasted_iota(jnp.int32, (tm, LANES), 1)
    ng = GATE_ONE_LANE
    f = _dot(xb, wf_ref[...]) + bf_ref[...]
    ubuf[hl:hl + tm, :] = _dot(xb, wu_ref[...])

    lf = (jnp.minimum(f, 0.0) - jnp.log1p(jnp.exp(-jnp.abs(f)))) * LOG2E
    l_hi, l_mid, l_lo = _split3(lf)
    zero = jnp.zeros_like(l_hi)
    pieces = jnp.where(lane < ng, l_hi,
                       jnp.where(lane < 2 * ng, l_mid, jnp.where(lane < 3 * ng, l_lo, zero)))
    row = lax.broadcasted_iota(jnp.int32, (tm, tm), 0)
    col = lax.broadcasted_iota(jnp.int32, (tm, tm), 1)
    tri = jnp.where(col <= row, 1.0, 0.0).astype(_BF16)
    cs = _dot(tri, pieces)
    q = _dot(xb, wq_ref[...]).astype(_BF16)

    s2 = ubuf[8:tm + 32, :] + ubuf[7:tm + 31, :]
    sb1[8:tm + 32, :] = s2[:, 128:512]
    s4 = sb1[16:tm + 32, :] + sb1[14:tm + 30, :]
    sb2[16:tm + 32, :] = s4[:, 128:384]
    s8 = sb2[24:tm + 32, :] + sb2[20:tm + 28, :]
    sb3[24:tm + 32, :] = s8[:, 128:256]
    s16 = sb3[32:tm + 32, :] + sb3[24:tm + 24, :]
    sums = (s2[24:, 0:128], s4[16:, 0:128], s8[8:, 0:128], s16)

    c = cs + pltpu.roll(cs, LANES - ng, axis=1) + pltpu.roll(cs, LANES - 2 * ng, axis=1)
    c = jnp.where(lane < ng, c, 0.0) + carry[7:8, :]
    carry[...] = c[tm - 8:tm, :]
    c_hi, c_mid, c_lo = _split3(c)
    one = jnp.where(lane == GATE_ONE_LANE, 1.0, 0.0).astype(_BF16)
    gate = jnp.where(lane < 8, c_hi, jnp.where(lane < 16, c_mid, jnp.where(lane < 24, c_lo, one)))
    gq = _dot(gate, eq_ref[...]).astype(_BF16)
    gk = _dot(gate, ek_ref[...]).astype(_BF16)
    k = _dot(xb, wk_ref[...]).astype(_BF16)

    pos = (i * tm + lax.broadcasted_iota(jnp.int32, (tm, 1), 0) + 1).astype(_F32)
    for g, win in enumerate(POOL_WINDOWS):
        sl = slice(g * POOL_GROUP_WIDTH, (g + 1) * POOL_GROUP_WIDTH)
        cnt = jnp.minimum(pos, float(win))
        d = sums[g] / cnt - ubuf[hl:hl + tm, sl]
        y = _dot(d.astype(_BF16), pw_ref[g]) * ps_ref[:, sl]
        pool_ref[:, sl] = y.astype(_BF16)

    vt = _dot_nt(wvt_ref[...], xb)
    pad_rows = VT_ROWS - HEAD_DIM
    ones_row = jnp.where(lax.broadcasted_iota(jnp.int32, (pad_rows, tm), 0) == 0, 1.0, 0.0)
    for h in range(N_HEADS):
        vt_ref[h, 0:HEAD_DIM, :] = vt[h * HEAD_DIM:(h + 1) * HEAD_DIM, :].astype(_BF16)
        vt_ref[h, HEAD_DIM:VT_ROWS, :] = ones_row.astype(_BF16)

    for p in range(N_PAIRS):
        src = slice(p * LANES, (p + 1) * LANES)
        qa_ref[:, p * PAIR_WIDTH:p * PAIR_WIDTH + LANES] = q[:, src]
        qa_ref[:, p * PAIR_WIDTH + LANES:(p + 1) * PAIR_WIDTH] = gq[:, src]
        ka_ref[:, p * PAIR_WIDTH:p * PAIR_WIDTH + LANES] = k[:, src]
        ka_ref[:, p * PAIR_WIDTH + LANES:(p + 1) * PAIR_WIDTH] = gk[:, src]


def _inproj(x, wu, wq, wk, wvt, wf, bf, pw, ps, eq, ek):
    b, s, d = x.shape
    tm = TM_IN
    const = lambda shape: pl.BlockSpec(shape, lambda bi, i: (0,) * len(shape))
    tile = lambda w: pl.BlockSpec((None, tm, w), lambda bi, i: (bi, i, 0))
    out_shape = [jax.ShapeDtypeStruct((b, s, POOL_WIDTH), _BF16),
                 jax.ShapeDtypeStruct((b, s, N_PAIRS * PAIR_WIDTH), _BF16),
                 jax.ShapeDtypeStruct((b, s, N_PAIRS * PAIR_WIDTH), _BF16),
                 jax.ShapeDtypeStruct((b, N_HEADS, s // tm, VT_ROWS, tm), _BF16)]
    out_specs = [tile(POOL_WIDTH), tile(N_PAIRS * PAIR_WIDTH), tile(N_PAIRS * PAIR_WIDTH),
                 pl.BlockSpec((None, N_HEADS, None, VT_ROWS, tm), lambda bi, i: (bi, 0, i, 0, 0))]
    return pl.pallas_call(
        _inproj_kernel,
        grid=(b, s // tm),
        in_specs=[tile(d), const(wu.shape), const(wq.shape), const(wk.shape), const(wvt.shape),
                  const(wf.shape), const(bf.shape), const(pw.shape), const(ps.shape),
                  const(eq.shape), const(ek.shape)],
        out_specs=out_specs,
        out_shape=out_shape,
        scratch_shapes=[pltpu.VMEM((tm + POOL_HALO, 512), _F32),
                        pltpu.VMEM((tm + POOL_HALO, 384), _F32),
                        pltpu.VMEM((tm + POOL_HALO, 256), _F32),
                        pltpu.VMEM((tm + POOL_HALO, 128), _F32),
                        pltpu.VMEM((8, LANES), _F32)],
        compiler_params=pltpu.CompilerParams(
            dimension_semantics=("arbitrary", "arbitrary"), vmem_limit_bytes=VMEM_LIMIT),
        name="inproj",
    )(x, wu, wq, wk, wvt, wf, bf, pw, ps, eq, ek)


def _attn_kernel(qa_ref, ka_ref, vt_ref, o_ref, qh_sc, m_sc, acc_sc, *rings):
    s_ring, mx_ring = rings[:ATT_RING], rings[ATT_RING:]
    i = pl.program_id(2)
    tq, tk, cw = T_Q, T_K, MXU_DIM
    g = GATE_COLS_PER_HEAD
    lane = lax.broadcasted_iota(jnp.int32, (tq, PAIR_WIDTH), 1)
    qf = qa_ref[...].astype(_F32)
    qh_sc[0] = jnp.where((lane < HEAD_DIM) | ((lane >= LANES) & (lane < LANES + g)),
                         qf, 0.0).astype(_BF16)
    qh_sc[1] = jnp.where(((lane >= HEAD_DIM) & (lane < LANES))
                         | ((lane >= LANES + g) & (lane < LANES + 2 * g)), qf, 0.0).astype(_BF16)
    m_sc[...] = jnp.full(m_sc.shape, NEG_INF, _F32)
    acc_sc[...] = jnp.zeros(acc_sc.shape, _F32)

    def produce(slot, blk, h, c, mask_off):
        kb = ka_ref[pl.ds(pl.multiple_of(blk * tk, tk), tk), :]
        s = _dot_nt(kb, qh_sc[h, c * cw:(c + 1) * cw, :])
        if mask_off is not None:
            kpos = lax.broadcasted_iota(jnp.int32, s.shape, 0) + mask_off
            qpos = lax.broadcasted_iota(jnp.int32, s.shape, 1) + c * cw
            s = jnp.where(kpos <= qpos, s, NEG_INF)
        s_ring[slot][...] = s
        mx_ring[slot][...] = jnp.max(s, axis=0, keepdims=True)

    def consume(slot, blk, h, c):
        cs = slice(c * cw, (c + 1) * cw)
        m_prev = m_sc[h, :, cs]
        m_new = jnp.maximum(m_prev, mx_ring[slot][...])
        alpha = jnp.exp2(m_prev - m_new)
        p = jnp.exp2(s_ring[slot][...] - m_new).astype(_BF16)
        acc_sc[h, :, cs] = alpha * acc_sc[h, :, cs] + _dot(vt_ref[h, blk], p)
        m_sc[h, :, cs] = m_new

    def run(entries, following, pos0):
        stream = entries + following[:ATT_LOOKAHEAD]
        for n, (blk, h, c, _) in enumerate(entries):
            produce((pos0 + n + ATT_LOOKAHEAD) % ATT_RING, *stream[n + ATT_LOOKAHEAD])
            consume((pos0 + n) % ATT_RING, blk, h, c)

    nc = tq // cw
    per_q = tq // tk
    units = [(h, c) for h in range(2) for c in range(nc)]
    full_block = lambda blk: [(blk, h, c, None) for h, c in units]
    cpb = tk // cw
    diag = []
    for r in reversed(range(per_q)):
        diag += [(per_q * i + r, h, c, r * tk if c < (r + 1) * cpb else None)
                 for h, c in units if c >= r * cpb]
    assert len(units) % ATT_RING == 0 and ATT_LOOKAHEAD < ATT_RING
    assert per_q % ATT_BLOCKS_PER_ITER == 0

    for n in range(ATT_LOOKAHEAD):
        produce(n, *diag[n])
    run(diag, full_block(0), 0)

    def body(j, carry):
        blocks = [ATT_BLOCKS_PER_ITER * j + r for r in range(ATT_BLOCKS_PER_ITER)]
        run(sum((full_block(blk) for blk in blocks), []),
            full_block(ATT_BLOCKS_PER_ITER * (j + 1)), len(diag))
        return carry

    lax.fori_loop(0, (per_q // ATT_BLOCKS_PER_ITER) * i, body, 0)

    outs = []
    for h in range(2):
        a = acc_sc[h]
        outs.append(a[0:HEAD_DIM, :] / a[HEAD_DIM:HEAD_DIM + 1, :])
    o_ref[...] = jnp.concatenate(outs, axis=0).T.astype(o_ref.dtype)


def _attention(qa, ka, vt):
    b, s, _ = qa.shape
    tq, tk = T_Q, T_K
    return pl.pallas_call(
        _attn_kernel,
        grid=(b, N_PAIRS, s // tq),
        in_specs=[pl.BlockSpec((None, tq, PAIR_WIDTH), lambda bi, p, i: (bi, i, p)),
                  pl.BlockSpec((None, s, PAIR_WIDTH), lambda bi, p, i: (bi, 0, p)),
                  pl.BlockSpec((None, 2, s // tk, VT_ROWS, tk), lambda bi, p, i: (bi, p, 0, 0, 0))],
        out_specs=pl.BlockSpec((None, tq, LANES), lambda bi, p, i: (bi, i, p)),
        out_shape=jax.ShapeDtypeStruct((b, s, ATTN_WIDTH), _BF16),
        scratch_shapes=([pltpu.VMEM((2, tq, PAIR_WIDTH), _BF16),
                         pltpu.VMEM((2, 1, tq), _F32),
                         pltpu.VMEM((2, VT_ROWS, tq), _F32)]
                        + [pltpu.VMEM((tk, MXU_DIM), _F32) for _ in range(ATT_RING)]
                        + [pltpu.VMEM((1, MXU_DIM), _F32) for _ in range(ATT_RING)]),
        compiler_params=pltpu.CompilerParams(
            dimension_semantics=("arbitrary", "arbitrary", "arbitrary"),
            vmem_limit_bytes=VMEM_LIMIT),
        name="fox_attn",
    )(qa, ka, vt)


def _layer_norm(z, g, b):
    mu = jnp.mean(z, axis=-1, keepdims=True)
    zc = z - mu
    var = jnp.mean(zc * zc, axis=-1, keepdims=True)
    return zc * lax.rsqrt(var + LN_EPS) * g + b


def _post_kernel(x_ref, pool_ref, attn_ref, wop_ref, woa_ref, g1_ref, b1_ref,
                 w1_ref, w2_ref, g2_ref, b2_ref, o_ref):
    sub = x_ref.shape[0] // POST_SUBTILES
    rows = [slice(n * sub, (n + 1) * sub) for n in range(POST_SUBTILES)]

    def out_proj(r):
        return _dot(pool_ref[r, :], wop_ref[...]) + _dot(attn_ref[r, :], woa_ref[...])

    def mlp_chunks(x1b, y, chunks):
        for c in chunks:
            sl = slice(c * FF_CHUNK, (c + 1) * FF_CHUNK)
            hid = jnp.maximum(_dot(x1b, w1_ref[:, sl]), 0.0)
            y = y + _dot((hid * hid).astype(_BF16), w2_ref[sl, :])
        return y

    def ln1(r, mix):
        return _layer_norm(ALPHA * x_ref[r, :] + mix, g1_ref[...], b1_ref[...])

    def ln2_store(r, x1, y):
        o_ref[r, :] = _layer_norm(ALPHA * x1 + y, g2_ref[...], b2_ref[...])

    first, rest = [0], list(range(1, D_FF // FF_CHUNK))
    mix = [out_proj(r) for r in rows]
    x1 = [ln1(rows[0], mix[0])] + [None] * (len(rows) - 1)
    y_prev = None
    for n, r in enumerate(rows):
        x1n_b = x1[n].astype(_BF16)
        y = mlp_chunks(x1n_b, jnp.zeros_like(x1[n]), first)
        if n + 1 < len(rows):
            x1[n + 1] = ln1(rows[n + 1], mix[n + 1])
        if n > 0:
            ln2_store(rows[n - 1], x1[n - 1], y_prev)
        y_prev = mlp_chunks(x1n_b, y, rest)
    ln2_store(rows[-1], x1[-1], y_prev)


def _post(x, pool, attn, wop, woa, g1, b1, w1, w2, g2, b2):
    t, d = x.shape
    tm = TM_POST
    const = lambda a: pl.BlockSpec(a.shape, lambda i: (0,) * a.ndim, pipeline_mode=pl.Buffered(1))
    tile = lambda w: pl.BlockSpec((tm, w), lambda i: (i, 0))
    return pl.pallas_call(
        _post_kernel,
        grid=(t // tm,),
        in_specs=[tile(d), tile(512), tile(512), const(wop), const(woa), const(g1), const(b1),
                  const(w1), const(w2), const(g2), const(b2)],
        out_specs=tile(d),
        out_shape=jax.ShapeDtypeStruct((t, d), _F32),
        compiler_params=pltpu.CompilerParams(
            dimension_semantics=("arbitrary",), vmem_limit_bytes=VMEM_LIMIT),
        name="post",
    )(x, pool, attn, wop, woa, g1, b1, w1, w2, g2, b2)


def _gate_placement():
    eq = np.zeros((LANES, N_PAIRS * LANES), np.float32)
    ek = np.zeros((LANES, N_PAIRS * LANES), np.float32)
    for h in range(N_HEADS):
        base = (h // 2) * LANES + (h % 2) * GATE_COLS_PER_HEAD
        for piece in range(3):
            eq[8 * piece + h, base + piece] = 1.0
            eq[GATE_ONE_LANE, base + 3 + piece] = 1.0
            ek[GATE_ONE_LANE, base + piece] = 1.0
            ek[8 * piece + h, base + 3 + piece] = -1.0
    return jnp.asarray(eq, _BF16), jnp.asarray(ek, _BF16)


def kernel(x, w_in, b_f, pool_w, pool_scale, w_out, ln1_g, ln1_b, w_mlp1, w_mlp2, ln2_g, ln2_b):
    b, s, d = x.shape
    assert s % T_Q == 0 and (b * s) % TM_POST == 0
    eq, ek = _gate_placement()
    o_q = POOL_WIDTH
    o_k = o_q + ATTN_WIDTH
    o_v = o_k + ATTN_WIDTH
    o_f = o_v + ATTN_WIDTH
    row = lambda a: a.reshape(1, -1)
    for l in range(DEPTH):
        w = w_in[l]
        wu = w[:, :o_q].astype(_BF16)
        wq = (w[:, o_q:o_k] * (HEAD_DIM ** -0.5 * LOG2E)).astype(_BF16)
        wk = w[:, o_k:o_v].astype(_BF16)
        wvt = w[:, o_v:o_f].T.astype(_BF16)
        wf = jnp.pad(jnp.tile(w[:, o_f:], (1, GATE_COPIES)),
                     ((0, 0), (0, LANES - GATE_COPIES * N_HEADS))).astype(_BF16)
        bf = jnp.pad(jnp.tile(b_f[l], GATE_COPIES), (0, LANES - GATE_COPIES * N_HEADS)).reshape(1, LANES)
        pool, qa, ka, vt = _inproj(x, wu, wq, wk, wvt, wf, bf, pool_w[l].astype(_BF16),
                                   row(pool_scale[l]), eq, ek)
        attn = _attention(qa, ka, vt)
        wo = w_out[l].astype(_BF16)
        x = _post(x.reshape(b * s, d), pool.reshape(b * s, -1), attn.reshape(b * s, -1),
                  wo[:POOL_WIDTH], wo[POOL_WIDTH:], row(ln1_g[l]), row(ln1_b[l]),
                  w_mlp1[l].astype(_BF16), w_mlp2[l].astype(_BF16),
                  row(ln2_g[l]), row(ln2_b[l])).reshape(b, s, d)
    return x
```

```python
import jax
import jax.numpy as jnp
import numpy as np
from jax import lax
from jax.experimental import pallas as pl
from jax.experimental.pallas import tpu as pltpu

D_MODEL = 1024
DEPTH = 2
POOL_WIDTH = 512
POOL_WINDOWS = (2, 4, 8, 16)
POOL_GROUP_WIDTH = 128
ATTN_WIDTH = 512
HEAD_DIM = 64
N_HEADS = 8
N_PAIRS = N_HEADS // 2
D_FF = 4 * D_MODEL
LN_EPS = 1e-5
NEG_INF = -1e30
ALPHA = float((2.0 * DEPTH) ** 0.25)
LOG2E = 1.4426950408889634

LANES = 128
MXU_DIM = 256
POOL_HALO = 32
GATE_ONE_LANE = 24
GATE_COPIES = 9
GATE_COLS_PER_HEAD = 6
PAIR_WIDTH = 2 * LANES
VT_ROWS = 80
STAT_Q2, STAT_K2, STAT_C_FIRST, STAT_C_LAST, STAT_ROWS = 0, 1, 2, 3, 4
PRUNE_MARGIN = 152.0
PRUNE_NORM_SAFETY = 1.02

T_Q = 2048
T_K = 512
ATT_BLOCKS_PER_ITER = 2
ATT_RING = 4
ATT_LOOKAHEAD = 3
TM_IN = T_K
TM_POST = 1024
POST_SUBTILES = 4
FF_CHUNK = 1024
VMEM_LIMIT = 56 * 1024 * 1024

_BF16 = jnp.bfloat16
_F32 = jnp.float32
_NT = (((1,), (1,)), ((), ()))


def _dot(a, b):
    return jnp.dot(a, b, preferred_element_type=_F32)


def _dot_nt(a, b):
    return lax.dot_general(a, b, _NT, preferred_element_type=_F32)


def _split3(x):
    hi = x.astype(_BF16)
    r = x - hi.astype(_F32)
    mid = r.astype(_BF16)
    lo = (r - mid.astype(_F32)).astype(_BF16)
    return hi, mid, lo


def _inproj_kernel(x_ref, wu_ref, wq_ref, wk_ref, wvt_ref, wf_ref, bf_ref, pw_ref, ps_ref,
                   eq_ref, ek_ref, hind_ref,
                   pool_ref, qa_ref, ka_ref, vt_ref, st_ref,
                   ubuf, sb1, sb2, sb3, carry):
    i = pl.program_id(1)
    tm = x_ref.shape[0]
    hl = POOL_HALO

    @pl.when(i == 0)
    def _():
        ubuf[0:hl, :] = jnp.zeros((hl, POOL_WIDTH), _F32)
        carry[...] = jnp.zeros(carry.shape, _F32)

    @pl.when(i > 0)
    def _():
        ubuf[0:hl, :] = ubuf[tm:tm + hl, :]

    xb = x_ref[...].astype(_BF16)

    lane = lax.broadcasted_iota(jnp.int32, (tm, LANES), 1)
    ng = GATE_ONE_LANE
    f = _dot(xb, wf_ref[...]) + bf_ref[...]
    ubuf[hl:hl + tm, :] = _dot(xb, wu_ref[...])

    lf = (jnp.minimum(f, 0.0) - jnp.log1p(jnp.exp(-jnp.abs(f)))) * LOG2E
    l_hi, l_mid, l_lo = _split3(lf)
    zero = jnp.zeros_like(l_hi)
    pieces = jnp.where(lane < ng, l_hi,
                       jnp.where(lane < 2 * ng, l_mid, jnp.where(lane < 3 * ng, l_lo, zero)))
    row = lax.broadcasted_iota(jnp.int32, (tm, tm), 0)
    col = lax.broadcasted_iota(jnp.int32, (tm, tm), 1)
    tri = jnp.where(col <= row, 1.0, 0.0).astype(_BF16)
    cs = _dot(tri, pieces)
    qf = _dot(xb, wq_ref[...])
    q = qf.astype(_BF16)

    s2 = ubuf[8:tm + 32, :] + ubuf[7:tm + 31, :]
    sb1[8:tm + 32, :] = s2[:, 128:512]
    s4 = sb1[16:tm + 32, :] + sb1[14:tm + 30, :]
    sb2[16:tm + 32, :] = s4[:, 128:384]
    s8 = sb2[24:tm + 32, :] + sb2[20:tm + 28, :]
    sb3[24:tm + 32, :] = s8[:, 128:256]
    s16 = sb3[32:tm + 32, :] + sb3[24:tm + 24, :]
    sums = (s2[24:, 0:128], s4[16:, 0:128], s8[8:, 0:128], s16)

    c = cs + pltpu.roll(cs, LANES - ng, axis=1) + pltpu.roll(cs, LANES - 2 * ng, axis=1)
    c = jnp.where(lane < ng, c, 0.0) + carry[7:8, :]
    carry[...] = c[tm - 8:tm, :]
    c_hi, c_mid, c_lo = _split3(c)
    one = jnp.where(lane == GATE_ONE_LANE, 1.0, 0.0).astype(_BF16)
    gate = jnp.where(lane < 8, c_hi, jnp.where(lane < 16, c_mid, jnp.where(lane < 24, c_lo, one)))
    gq = _dot(gate, eq_ref[...]).astype(_BF16)
    gk = _dot(gate, ek_ref[...]).astype(_BF16)
    kf = _dot(xb, wk_ref[...])
    k = kf.astype(_BF16)

    st_ref[STAT_Q2:STAT_Q2 + 1, :] = jnp.max(
        _dot((qf * qf).astype(_BF16), hind_ref[...]), axis=0, keepdims=True)
    st_ref[STAT_K2:STAT_K2 + 1, :] = jnp.max(
        _dot((kf * kf).astype(_BF16), hind_ref[...]), axis=0, keepdims=True)
    st_ref[STAT_C_FIRST:STAT_C_FIRST + 1, :] = c[0:1, :]
    st_ref[STAT_C_LAST:STAT_C_LAST + 1, :] = c[tm - 1:tm, :]
    st_ref[STAT_ROWS:8, :] = jnp.zeros((8 - STAT_ROWS, LANES), _F32)

    pos = (i * tm + lax.broadcasted_iota(jnp.int32, (tm, 1), 0) + 1).astype(_F32)
    for g, win in enumerate(POOL_WINDOWS):
        sl = slice(g * POOL_GROUP_WIDTH, (g + 1) * POOL_GROUP_WIDTH)
        cnt = jnp.minimum(pos, float(win))
        d = sums[g] / cnt - ubuf[hl:hl + tm, sl]
        y = _dot(d.astype(_BF16), pw_ref[g]) * ps_ref[:, sl]
        pool_ref[:, sl] = y.astype(_BF16)

    vt = _dot_nt(wvt_ref[...], xb)
    pad_rows = VT_ROWS - HEAD_DIM
    ones_row = jnp.where(lax.broadcasted_iota(jnp.int32, (pad_rows, tm), 0) == 0, 1.0, 0.0)
    for h in range(N_HEADS):
        vt_ref[h, 0:HEAD_DIM, :] = vt[h * HEAD_DIM:(h + 1) * HEAD_DIM, :].astype(_BF16)
        vt_ref[h, HEAD_DIM:VT_ROWS, :] = ones_row.astype(_BF16)

    for p in range(N_PAIRS):
        src = slice(p * LANES, (p + 1) * LANES)
        qa_ref[:, p * PAIR_WIDTH:p * PAIR_WIDTH + LANES] = q[:, src]
        qa_ref[:, p * PAIR_WIDTH + LANES:(p + 1) * PAIR_WIDTH] = gq[:, src]
        ka_ref[:, p * PAIR_WIDTH:p * PAIR_WIDTH + LANES] = k[:, src]
        ka_ref[:, p * PAIR_WIDTH + LANES:(p + 1) * PAIR_WIDTH] = gk[:, src]


def _inproj(x, wu, wq, wk, wvt, wf, bf, pw, ps, eq, ek, hind):
    b, s, d = x.shape
    tm = TM_IN
    const = lambda shape: pl.BlockSpec(shape, lambda bi, i: (0,) * len(shape))
    tile = lambda w: pl.BlockSpec((None, tm, w), lambda bi, i: (bi, i, 0))
    out_shape = [jax.ShapeDtypeStruct((b, s, POOL_WIDTH), _BF16),
                 jax.ShapeDtypeStruct((b, s, N_PAIRS * PAIR_WIDTH), _BF16),
                 jax.ShapeDtypeStruct((b, s, N_PAIRS * PAIR_WIDTH), _BF16),
                 jax.ShapeDtypeStruct((b, N_HEADS, s // tm, VT_ROWS, tm), _BF16),
                 jax.ShapeDtypeStruct((b, s // tm, 8, LANES), _F32)]
    out_specs = [tile(POOL_WIDTH), tile(N_PAIRS * PAIR_WIDTH), tile(N_PAIRS * PAIR_WIDTH),
                 pl.BlockSpec((None, N_HEADS, None, VT_ROWS, tm), lambda bi, i: (bi, 0, i, 0, 0)),
                 pl.BlockSpec((None, None, 8, LANES), lambda bi, i: (bi, i, 0, 0))]
    return pl.pallas_call(
        _inproj_kernel,
        grid=(b, s // tm),
        in_specs=[tile(d), const(wu.shape), const(wq.shape), const(wk.shape), const(wvt.shape),
                  const(wf.shape), const(bf.shape), const(pw.shape), const(ps.shape),
                  const(eq.shape), const(ek.shape), const(hind.shape)],
        out_specs=out_specs,
        out_shape=out_shape,
        scratch_shapes=[pltpu.VMEM((tm + POOL_HALO, 512), _F32),
                        pltpu.VMEM((tm + POOL_HALO, 384), _F32),
                        pltpu.VMEM((tm + POOL_HALO, 256), _F32),
                        pltpu.VMEM((tm + POOL_HALO, 128), _F32),
                        pltpu.VMEM((8, LANES), _F32)],
        compiler_params=pltpu.CompilerParams(
            dimension_semantics=("arbitrary", "arbitrary"), vmem_limit_bytes=VMEM_LIMIT),
        name="inproj",
    )(x, wu, wq, wk, wvt, wf, bf, pw, ps, eq, ek, hind)


def _attn_kernel(qa_ref, ka_ref, vt_ref, st_ref, o_ref, qh_sc, m_sc, acc_sc, *rings):
    s_ring, mx_ring = rings[:ATT_RING], rings[ATT_RING:]
    i = pl.program_id(2)
    tq, tk, cw = T_Q, T_K, MXU_DIM
    g = GATE_COLS_PER_HEAD
    lane = lax.broadcasted_iota(jnp.int32, (tq, PAIR_WIDTH), 1)
    qf = qa_ref[...].astype(_F32)
    keep0 = (lane < HEAD_DIM) | ((lane >= LANES) & (lane < LANES + g))
    keep1 = ((lane >= HEAD_DIM) & (lane < LANES)) | ((lane >= LANES + g) & (lane < LANES + 2 * g))
    qh_sc[0] = jnp.where(keep0, qf, 0.0).astype(_BF16)
    qh_sc[1] = jnp.where(keep1, qf, 0.0).astype(_BF16)
    m_sc[...] = jnp.full(m_sc.shape, NEG_INF, _F32)
    acc_sc[...] = jnp.zeros(acc_sc.shape, _F32)

    def produce(slot, blk, h, c, mask_off, rows):
        kb = ka_ref[pl.ds(pl.multiple_of(blk * tk, tk), rows), :]
        s = _dot_nt(kb, qh_sc[h, c * cw:(c + 1) * cw, :])
        if mask_off is not None:
            kpos = lax.broadcasted_iota(jnp.int32, s.shape, 0) + mask_off
            qpos = lax.broadcasted_iota(jnp.int32, s.shape, 1) + c * cw
            s = jnp.where(kpos <= qpos, s, NEG_INF)
        s_ring[slot][0:rows, :] = s
        mx_ring[slot][...] = jnp.max(s, axis=0, keepdims=True)

    def consume(slot, blk, h, c, mask_off, rows):
        cs = slice(c * cw, (c + 1) * cw)
        m_prev = m_sc[h, :, cs]
        m_new = jnp.maximum(m_prev, mx_ring[slot][...])
        alpha = jnp.exp2(m_prev - m_new)
        p = jnp.exp2(s_ring[slot][0:rows, :] - m_new).astype(_BF16)
        acc_sc[h, :, cs] = alpha * acc_sc[h, :, cs] + _dot(vt_ref[h, blk, :, 0:rows], p)
        m_sc[h, :, cs] = m_new

    def run(entries, following):
        stream = entries + following[:ATT_LOOKAHEAD]
        for n, unit in enumerate(entries):
            if n + ATT_LOOKAHEAD < len(stream):
                produce((n + ATT_LOOKAHEAD) % ATT_RING, *stream[n + ATT_LOOKAHEAD])
            consume(n % ATT_RING, *unit)

    nc = tq // cw
    per_q = tq // tk
    cpb = tk // cw
    units = [(h, c) for h in range(2) for c in range(nc)]
    full_block = lambda blk: [(blk, h, c, None, tk) for h, c in units]
    diag = []
    for r in reversed(range(per_q)):
        for h, c in units:
            if c >= (r + 1) * cpb:
                diag.append((per_q * i + r, h, c, None, tk))
            elif c >= r * cpb:
                diag.append((per_q * i + r, h, c, r * tk, (c - r * cpb + 1) * cw))
    assert len(units) % ATT_RING == 0 and len(diag) % ATT_RING == 0 and ATT_LOOKAHEAD < ATT_RING
    assert per_q % ATT_BLOCKS_PER_ITER == 0

    n_iters = (per_q // ATT_BLOCKS_PER_ITER) * i
    tiles_per_q = tq // TM_IN
    tiles_per_iter = ATT_BLOCKS_PER_ITER * tk // TM_IN
    n_tiles = st_ref.shape[1]
    trow = lax.broadcasted_iota(jnp.int32, (n_tiles, LANES), 0)
    in_q = (trow >= i * tiles_per_q) & (trow < (i + 1) * tiles_per_q)
    q2 = jnp.max(jnp.where(in_q, st_ref[STAT_Q2], 0.0), axis=0, keepdims=True)
    k2 = jnp.max(st_ref[STAT_K2], axis=0, keepdims=True)
    bound = PRUNE_NORM_SAFETY * jnp.sqrt(q2 * k2)
    c_t0 = jnp.max(jnp.where(trow == i * tiles_per_q, st_ref[STAT_C_FIRST], -jnp.inf),
                   axis=0, keepdims=True)
    prunable = (c_t0 - st_ref[STAT_C_LAST]) <= -(2.0 * bound + PRUNE_MARGIN)
    group_end = (trow + 1) % tiles_per_iter == 0
    count = jnp.sum(jnp.where(prunable & group_end, 1.0, 0.0), axis=0, keepdims=True)
    head_lane = lax.broadcasted_iota(jnp.int32, (1, LANES), 1) - 2 * pl.program_id(1)
    n_skip = jnp.minimum(jnp.sum(jnp.where(head_lane == 0, count, 0.0)),
                         jnp.sum(jnp.where(head_lane == 1, count, 0.0))).astype(jnp.int32)
    n_skip = jnp.minimum(n_skip, n_iters)

    for n in range(ATT_LOOKAHEAD):
        produce(n, *diag[n])
    run(diag, full_block(ATT_BLOCKS_PER_ITER * n_skip))

    def body(j, carry):
        blocks = [ATT_BLOCKS_PER_ITER * j + r for r in range(ATT_BLOCKS_PER_ITER)]
        run(sum((full_block(blk) for blk in blocks), []), full_block(ATT_BLOCKS_PER_ITER * (j + 1)))
        return carry

    lax.fori_loop(n_skip, n_iters, body, 0)

    outs = []
    for h in range(2):
        a = acc_sc[h]
        outs.append(a[0:HEAD_DIM, :] / a[HEAD_DIM:HEAD_DIM + 1, :])
    o_ref[...] = jnp.concatenate(outs, axis=0).T.astype(o_ref.dtype)


def _attention(qa, ka, vt, stats):
    b, s, _ = qa.shape
    tq, tk = T_Q, T_K
    return pl.pallas_call(
        _attn_kernel,
        grid=(b, N_PAIRS, s // tq),
        in_specs=[pl.BlockSpec((None, tq, PAIR_WIDTH), lambda bi, p, i: (bi, i, p)),
                  pl.BlockSpec((None, s, PAIR_WIDTH), lambda bi, p, i: (bi, 0, p)),
                  pl.BlockSpec((None, 2, s // tk, VT_ROWS, tk), lambda bi, p, i: (bi, p, 0, 0, 0)),
                  pl.BlockSpec((None,) + stats.shape[1:], lambda bi, p, i: (bi, 0, 0, 0))],
        out_specs=pl.BlockSpec((None, tq, LANES), lambda bi, p, i: (bi, i, p)),
        out_shape=jax.ShapeDtypeStruct((b, s, ATTN_WIDTH), _BF16),
        scratch_shapes=([pltpu.VMEM((2, tq, PAIR_WIDTH), _BF16),
                         pltpu.VMEM((2, 1, tq), _F32),
                         pltpu.VMEM((2, VT_ROWS, tq), _F32)]
                        + [pltpu.VMEM((tk, MXU_DIM), _F32) for _ in range(ATT_RING)]
                        + [pltpu.VMEM((1, MXU_DIM), _F32) for _ in range(ATT_RING)]),
        compiler_params=pltpu.CompilerParams(
            dimension_semantics=("arbitrary", "arbitrary", "arbitrary"),
            vmem_limit_bytes=VMEM_LIMIT),
        name="fox_attn",
    )(qa, ka, vt, stats)


def _layer_norm(z, g, b):
    mu = jnp.mean(z, axis=-1, keepdims=True)
    zc = z - mu
    var = jnp.mean(zc * zc, axis=-1, keepdims=True)
    return zc * lax.rsqrt(var + LN_EPS) * g + b


def _post_kernel(x_ref, pool_ref, attn_ref, wop_ref, woa_ref, g1_ref, b1_ref,
                 w1_ref, w2_ref, g2_ref, b2_ref, o_ref):
    sub = x_ref.shape[0] // POST_SUBTILES
    rows = [slice(n * sub, (n + 1) * sub) for n in range(POST_SUBTILES)]

    def out_proj(r):
        return _dot(pool_ref[r, :], wop_ref[...]) + _dot(attn_ref[r, :], woa_ref[...])

    def mlp_chunks(x1b, y, chunks):
        for c in chunks:
            sl = slice(c * FF_CHUNK, (c + 1) * FF_CHUNK)
            hid = jnp.maximum(_dot(x1b, w1_ref[:, sl]), 0.0)
            y = y + _dot((hid * hid).astype(_BF16), w2_ref[sl, :])
        return y

    def ln1(r, mix):
        return _layer_norm(ALPHA * x_ref[r, :] + mix, g1_ref[...], b1_ref[...])

    def ln2_store(r, x1, y):
        o_ref[r, :] = _layer_norm(ALPHA * x1 + y, g2_ref[...], b2_ref[...])

    first, rest = [0], list(range(1, D_FF // FF_CHUNK))
    mix = [out_proj(r) for r in rows]
    x1 = [ln1(rows[0], mix[0])] + [None] * (len(rows) - 1)
    y_prev = None
    for n, r in enumerate(rows):
        x1n_b = x1[n].astype(_BF16)
        y = mlp_chunks(x1n_b, jnp.zeros_like(x1[n]), first)
        if n + 1 < len(rows):
            x1[n + 1] = ln1(rows[n + 1], mix[n + 1])
        if n > 0:
            ln2_store(rows[n - 1], x1[n - 1], y_prev)
        y_prev = mlp_chunks(x1n_b, y, rest)
    ln2_store(rows[-1], x1[-1], y_prev)


def _post(x, pool, attn, wop, woa, g1, b1, w1, w2, g2, b2):
    t, d = x.shape
    tm = TM_POST
    const = lambda a: pl.BlockSpec(a.shape, lambda i: (0,) * a.ndim, pipeline_mode=pl.Buffered(1))
    tile = lambda w: pl.BlockSpec((tm, w), lambda i: (i, 0))
    return pl.pallas_call(
        _post_kernel,
        grid=(t // tm,),
        in_specs=[tile(d), tile(512), tile(512), const(wop), const(woa), const(g1), const(b1),
                  const(w1), const(w2), const(g2), const(b2)],
        out_specs=tile(d),
        out_shape=jax.ShapeDtypeStruct((t, d), _F32),
        compiler_params=pltpu.CompilerParams(
            dimension_semantics=("arbitrary",), vmem_limit_bytes=VMEM_LIMIT),
        name="post",
    )(x, pool, attn, wop, woa, g1, b1, w1, w2, g2, b2)


def _gate_placement():
    eq = np.zeros((LANES, N_PAIRS * LANES), np.float32)
    ek = np.zeros((LANES, N_PAIRS * LANES), np.float32)
    for h in range(N_HEADS):
        base = (h // 2) * LANES + (h % 2) * GATE_COLS_PER_HEAD
        for piece in range(3):
            eq[8 * piece + h, base + piece] = 1.0
            eq[GATE_ONE_LANE, base + 3 + piece] = 1.0
            ek[GATE_ONE_LANE, base + piece] = 1.0
            ek[8 * piece + h, base + 3 + piece] = -1.0
    return jnp.asarray(eq, _BF16), jnp.asarray(ek, _BF16)


def kernel(x, w_in, b_f, pool_w, pool_scale, w_out, ln1_g, ln1_b, w_mlp1, w_mlp2, ln2_g, ln2_b):
    b, s, d = x.shape
    assert s % T_Q == 0 and (b * s) % TM_POST == 0
    eq, ek = _gate_placement()
    hind = jnp.asarray(np.arange(ATTN_WIDTH)[:, None] // HEAD_DIM == np.arange(LANES)[None, :], _BF16)
    o_q = POOL_WIDTH
    o_k = o_q + ATTN_WIDTH
    o_v = o_k + ATTN_WIDTH
    o_f = o_v + ATTN_WIDTH
    row = lambda a: a.reshape(1, -1)
    for l in range(DEPTH):
        w = w_in[l]
        wu = w[:, :o_q].astype(_BF16)
        wq = (w[:, o_q:o_k] * (HEAD_DIM ** -0.5 * LOG2E)).astype(_BF16)
        wk = w[:, o_k:o_v].astype(_BF16)
        wvt = w[:, o_v:o_f].T.astype(_BF16)
        wf = jnp.pad(jnp.tile(w[:, o_f:], (1, GATE_COPIES)),
                     ((0, 0), (0, LANES - GATE_COPIES * N_HEADS))).astype(_BF16)
        bf = jnp.pad(jnp.tile(b_f[l], GATE_COPIES), (0, LANES - GATE_COPIES * N_HEADS)).reshape(1, LANES)
        pool, qa, ka, vt, stats = _inproj(x, wu, wq, wk, wvt, wf, bf, pool_w[l].astype(_BF16),
                                          row(pool_scale[l]), eq, ek, hind)
        attn = _attention(qa, ka, vt, stats.transpose(0, 2, 1, 3))
        wo = w_out[l].astype(_BF16)
        x = _post(x.reshape(b * s, d), pool.reshape(b * s, -1), attn.reshape(b * s, -1),
                  wo[:POOL_WIDTH], wo[POOL_WIDTH:], row(ln1_g[l]), row(ln1_b[l]),
                  w_mlp1[l].astype(_BF16), w_mlp2[l].astype(_BF16),
                  row(ln2_g[l]), row(ln2_b[l])).reshape(b, s, d)
    return x
```

```python
import jax
import jax.numpy as jnp
import numpy as np
from jax import lax
from jax.experimental import pallas as pl
from jax.experimental.pallas import tpu as pltpu

D_MODEL = 1024
DEPTH = 2
POOL_WIDTH = 512
POOL_WINDOWS = (2, 4, 8, 16)
POOL_GROUP_WIDTH = 128
ATTN_WIDTH = 512
HEAD_DIM = 64
N_HEADS = 8
N_PAIRS = N_HEADS // 2
D_FF = 4 * D_MODEL
LN_EPS = 1e-5
NEG_INF = -1e30
ALPHA = float((2.0 * DEPTH) ** 0.25)
LOG2E = 1.4426950408889634

LANES = 128
MXU_DIM = 256
POOL_HALO = 32
GATE_ONE_LANE = 24
GATE_COPIES = 9
GATE_COLS_PER_HEAD = 6
PAIR_WIDTH = 2 * LANES
VT_ROWS = 80
STAT_Q2, STAT_K2, STAT_C_FIRST, STAT_C_LAST, STAT_ROWS = 0, 1, 2, 3, 4
PRUNE_MARGIN = 152.0
PRUNE_NORM_SAFETY = 1.02

T_Q = 2048
T_K = 512
ATT_BLOCKS_PER_ITER = 1
ATT_RING = 4
ATT_LOOKAHEAD = 3
TM_IN = T_K
TM_POST = 1024
POST_SUBTILES = 4
FF_CHUNK = 1024
VMEM_LIMIT = 56 * 1024 * 1024

_BF16 = jnp.bfloat16
_F32 = jnp.float32
_NT = (((1,), (1,)), ((), ()))


def _dot(a, b):
    return jnp.dot(a, b, preferred_element_type=_F32)


def _dot_nt(a, b):
    return lax.dot_general(a, b, _NT, preferred_element_type=_F32)


def _split3(x):
    hi = x.astype(_BF16)
    r = x - hi.astype(_F32)
    mid = r.astype(_BF16)
    lo = (r - mid.astype(_F32)).astype(_BF16)
    return hi, mid, lo


def _inproj_kernel(x_ref, wu_ref, wq_ref, wk_ref, wvt_ref, wf_ref, bf_ref, pw_ref, ps_ref,
                   eq_ref, ek_ref, hind_ref,
                   pool_ref, qa_ref, ka_ref, vt_ref, st_ref,
                   ubuf, sb1, sb2, sb3, carry):
    i = pl.program_id(1)
    tm = x_ref.shape[0]
    hl = POOL_HALO

    @pl.when(i == 0)
    def _():
        ubuf[0:hl, :] = jnp.zeros((hl, POOL_WIDTH), _F32)
        carry[...] = jnp.zeros(carry.shape, _F32)

    @pl.when(i > 0)
    def _():
        ubuf[0:hl, :] = ubuf[tm:tm + hl, :]

    xb = x_ref[...].astype(_BF16)

    lane = lax.broadcasted_iota(jnp.int32, (tm, LANES), 1)
    ng = GATE_ONE_LANE
    f = _dot(xb, wf_ref[...]) + bf_ref[...]
    ubuf[hl:hl + tm, :] = _dot(xb, wu_ref[...])

    lf = (jnp.minimum(f, 0.0) - jnp.log1p(jnp.exp(-jnp.abs(f)))) * LOG2E
    l_hi, l_mid, l_lo = _split3(lf)
    zero = jnp.zeros_like(l_hi)
    pieces = jnp.where(lane < ng, l_hi,
                       jnp.where(lane < 2 * ng, l_mid, jnp.where(lane < 3 * ng, l_lo, zero)))
    row = lax.broadcasted_iota(jnp.int32, (tm, tm), 0)
    col = lax.broadcasted_iota(jnp.int32, (tm, tm), 1)
    tri = jnp.where(col <= row, 1.0, 0.0).astype(_BF16)
    cs = _dot(tri, pieces)
    qf = _dot(xb, wq_ref[...])
    q = qf.astype(_BF16)

    s2 = ubuf[8:tm + 32, :] + ubuf[7:tm + 31, :]
    sb1[8:tm + 32, :] = s2[:, 128:512]
    s4 = sb1[16:tm + 32, :] + sb1[14:tm + 30, :]
    sb2[16:tm + 32, :] = s4[:, 128:384]
    s8 = sb2[24:tm + 32, :] + sb2[20:tm + 28, :]
    sb3[24:tm + 32, :] = s8[:, 128:256]
    s16 = sb3[32:tm + 32, :] + sb3[24:tm + 24, :]
    sums = (s2[24:, 0:128], s4[16:, 0:128], s8[8:, 0:128], s16)

    c = cs + pltpu.roll(cs, LANES - ng, axis=1) + pltpu.roll(cs, LANES - 2 * ng, axis=1)
    c = jnp.where(lane < ng, c, 0.0) + carry[7:8, :]
    carry[...] = c[tm - 8:tm, :]
    c_hi, c_mid, c_lo = _split3(c)
    one = jnp.where(lane == GATE_ONE_LANE, 1.0, 0.0).astype(_BF16)
    gate = jnp.where(lane < 8, c_hi, jnp.where(lane < 16, c_mid, jnp.where(lane < 24, c_lo, one)))
    gq = _dot(gate, eq_ref[...]).astype(_BF16)
    gk = _dot(gate, ek_ref[...]).astype(_BF16)
    kf = _dot(xb, wk_ref[...])
    k = kf.astype(_BF16)

    st_ref[STAT_Q2:STAT_Q2 + 1, :] = jnp.max(
        _dot((qf * qf).astype(_BF16), hind_ref[...]), axis=0, keepdims=True)
    st_ref[STAT_K2:STAT_K2 + 1, :] = jnp.max(
        _dot((kf * kf).astype(_BF16), hind_ref[...]), axis=0, keepdims=True)
    st_ref[STAT_C_FIRST:STAT_C_FIRST + 1, :] = c[0:1, :]
    st_ref[STAT_C_LAST:STAT_C_LAST + 1, :] = c[tm - 1:tm, :]
    st_ref[STAT_ROWS:8, :] = jnp.zeros((8 - STAT_ROWS, LANES), _F32)

    pos = (i * tm + lax.broadcasted_iota(jnp.int32, (tm, 1), 0) + 1).astype(_F32)
    for g, win in enumerate(POOL_WINDOWS):
        sl = slice(g * POOL_GROUP_WIDTH, (g + 1) * POOL_GROUP_WIDTH)
        cnt = jnp.minimum(pos, float(win))
        d = sums[g] / cnt - ubuf[hl:hl + tm, sl]
        y = _dot(d.astype(_BF16), pw_ref[g]) * ps_ref[:, sl]
        pool_ref[:, sl] = y.astype(_BF16)

    vt = _dot_nt(wvt_ref[...], xb)
    pad_rows = VT_ROWS - HEAD_DIM
    ones_row = jnp.where(lax.broadcasted_iota(jnp.int32, (pad_rows, tm), 0) == 0, 1.0, 0.0)
    for h in range(N_HEADS):
        vt_ref[h, 0:HEAD_DIM, :] = vt[h * HEAD_DIM:(h + 1) * HEAD_DIM, :].astype(_BF16)
        vt_ref[h, HEAD_DIM:VT_ROWS, :] = ones_row.astype(_BF16)

    for p in range(N_PAIRS):
        src = slice(p * LANES, (p + 1) * LANES)
        qa_ref[:, p * PAIR_WIDTH:p * PAIR_WIDTH + LANES] = q[:, src]
        qa_ref[:, p * PAIR_WIDTH + LANES:(p + 1) * PAIR_WIDTH] = gq[:, src]
        ka_ref[:, p * PAIR_WIDTH:p * PAIR_WIDTH + LANES] = k[:, src]
        ka_ref[:, p * PAIR_WIDTH + LANES:(p + 1) * PAIR_WIDTH] = gk[:, src]


def _inproj(x, wu, wq, wk, wvt, wf, bf, pw, ps, eq, ek, hind):
    b, s, d = x.shape
    tm = TM_IN
    const = lambda shape: pl.BlockSpec(shape, lambda bi, i: (0,) * len(shape))
    tile = lambda w: pl.BlockSpec((None, tm, w), lambda bi, i: (bi, i, 0))
    out_shape = [jax.ShapeDtypeStruct((b, s, POOL_WIDTH), _BF16),
                 jax.ShapeDtypeStruct((b, s, N_PAIRS * PAIR_WIDTH), _BF16),
                 jax.ShapeDtypeStruct((b, s, N_PAIRS * PAIR_WIDTH), _BF16),
                 jax.ShapeDtypeStruct((b, N_HEADS, s // tm, VT_ROWS, tm), _BF16),
                 jax.ShapeDtypeStruct((b, s // tm, 8, LANES), _F32)]
    out_specs = [tile(POOL_WIDTH), tile(N_PAIRS * PAIR_WIDTH), tile(N_PAIRS * PAIR_WIDTH),
                 pl.BlockSpec((None, N_HEADS, None, VT_ROWS, tm), lambda bi, i: (bi, 0, i, 0, 0)),
                 pl.BlockSpec((None, None, 8, LANES), lambda bi, i: (bi, i, 0, 0))]
    return pl.pallas_call(
        _inproj_kernel,
        grid=(b, s // tm),
        in_specs=[tile(d), const(wu.shape), const(wq.shape), const(wk.shape), const(wvt.shape),
                  const(wf.shape), const(bf.shape), const(pw.shape), const(ps.shape),
                  const(eq.shape), const(ek.shape), const(hind.shape)],
        out_specs=out_specs,
        out_shape=out_shape,
        scratch_shapes=[pltpu.VMEM((tm + POOL_HALO, 512), _F32),
                        pltpu.VMEM((tm + POOL_HALO, 384), _F32),
                        pltpu.VMEM((tm + POOL_HALO, 256), _F32),
                        pltpu.VMEM((tm + POOL_HALO, 128), _F32),
                        pltpu.VMEM((8, LANES), _F32)],
        compiler_params=pltpu.CompilerParams(
            dimension_semantics=("arbitrary", "arbitrary"), vmem_limit_bytes=VMEM_LIMIT),
        name="inproj",
    )(x, wu, wq, wk, wvt, wf, bf, pw, ps, eq, ek, hind)


def _attn_kernel(qa_ref, ka_ref, vt_ref, st_ref, o_ref, qh_sc, m_sc, acc_sc, *rings):
    s_ring, mx_ring = rings[:ATT_RING], rings[ATT_RING:]
    i = pl.program_id(2)
    tq, tk, cw = T_Q, T_K, MXU_DIM
    g = GATE_COLS_PER_HEAD
    lane = lax.broadcasted_iota(jnp.int32, (tq, PAIR_WIDTH), 1)
    qf = qa_ref[...].astype(_F32)
    keep0 = (lane < HEAD_DIM) | ((lane >= LANES) & (lane < LANES + g))
    keep1 = ((lane >= HEAD_DIM) & (lane < LANES)) | ((lane >= LANES + g) & (lane < LANES + 2 * g))
    qh_sc[0] = jnp.where(keep0, qf, 0.0).astype(_BF16)
    qh_sc[1] = jnp.where(keep1, qf, 0.0).astype(_BF16)
    m_sc[...] = jnp.full(m_sc.shape, NEG_INF, _F32)
    acc_sc[...] = jnp.zeros(acc_sc.shape, _F32)

    def produce(slot, blk, h, c, mask_off, rows):
        kb = ka_ref[pl.ds(pl.multiple_of(blk * tk, tk), rows), :]
        s = _dot_nt(kb, qh_sc[h, c * cw:(c + 1) * cw, :])
        if mask_off is not None:
            kpos = lax.broadcasted_iota(jnp.int32, s.shape, 0) + mask_off
            qpos = lax.broadcasted_iota(jnp.int32, s.shape, 1) + c * cw
            s = jnp.where(kpos <= qpos, s, NEG_INF)
        s_ring[slot][0:rows, :] = s
        mx_ring[slot][...] = jnp.max(s, axis=0, keepdims=True)

    def consume(slot, blk, h, c, mask_off, rows):
        cs = slice(c * cw, (c + 1) * cw)
        m_prev = m_sc[h, :, cs]
        m_new = jnp.maximum(m_prev, mx_ring[slot][...])
        alpha = jnp.exp2(m_prev - m_new)
        p = jnp.exp2(s_ring[slot][0:rows, :] - m_new).astype(_BF16)
        acc_sc[h, :, cs] = alpha * acc_sc[h, :, cs] + _dot(vt_ref[h, blk, :, 0:rows], p)
        m_sc[h, :, cs] = m_new

    def run(entries, following, pos0=0):
        stream = entries + following[:ATT_LOOKAHEAD]
        for n, unit in enumerate(entries):
            produce((pos0 + n + ATT_LOOKAHEAD) % ATT_RING, *stream[n + ATT_LOOKAHEAD])
            consume((pos0 + n) % ATT_RING, *unit)

    nc = tq // cw
    per_q = tq // tk
    cpb = tk // cw
    units = [(h, c) for h in range(2) for c in range(nc)]
    full_block = lambda blk: [(blk, h, c, None, tk) for h, c in units]
    diag = []
    for r in reversed(range(per_q)):
        for h, c in units:
            if c >= (r + 1) * cpb:
                diag.append((per_q * i + r, h, c, None, tk))
            elif c >= r * cpb:
                diag.append((per_q * i + r, h, c, r * tk, (c - r * cpb + 1) * cw))
    assert len(units) % ATT_RING == 0 and len(diag) % ATT_RING == 0 and ATT_LOOKAHEAD < ATT_RING
    assert per_q % ATT_BLOCKS_PER_ITER == 0

    split = len(diag) - 2 * ATT_LOOKAHEAD
    assert {(h, c) for _, h, c, _, _ in diag[:split]} == set(units)
    for n in range(ATT_LOOKAHEAD):
        produce(n, *diag[n])
    run(diag[:split], diag[split:], 0)

    n_iters = (per_q // ATT_BLOCKS_PER_ITER) * i
    tiles_per_q = tq // TM_IN
    tiles_per_iter = ATT_BLOCKS_PER_ITER * tk // TM_IN
    n_tiles = st_ref.shape[1]
    trow = lax.broadcasted_iota(jnp.int32, (n_tiles, LANES), 0)
    in_q = (trow >= i * tiles_per_q) & (trow < (i + 1) * tiles_per_q)
    q2 = jnp.max(jnp.where(in_q, st_ref[STAT_Q2], 0.0), axis=0, keepdims=True)
    k2 = jnp.max(st_ref[STAT_K2], axis=0, keepdims=True)
    bound = PRUNE_NORM_SAFETY * jnp.sqrt(q2 * k2)
    c_t0 = jnp.max(jnp.where(trow == i * tiles_per_q, st_ref[STAT_C_FIRST], -jnp.inf),
                   axis=0, keepdims=True)
    m_min = jnp.min(m_sc[...])
    prunable = (c_t0 - st_ref[STAT_C_LAST]) + bound <= m_min - PRUNE_MARGIN
    group_end = (trow + 1) % tiles_per_iter == 0
    count = jnp.sum(jnp.where(prunable & group_end, 1.0, 0.0), axis=0, keepdims=True)
    head_lane = lax.broadcasted_iota(jnp.int32, (1, LANES), 1) - 2 * pl.program_id(1)
    n_skip = jnp.minimum(jnp.sum(jnp.where(head_lane == 0, count, 0.0)),
                         jnp.sum(jnp.where(head_lane == 1, count, 0.0))).astype(jnp.int32)
    n_skip = jnp.minimum(n_skip, n_iters)

    run(diag[split:], full_block(ATT_BLOCKS_PER_ITER * n_skip), split)

    def body(j, carry):
        blocks = [ATT_BLOCKS_PER_ITER * j + r for r in range(ATT_BLOCKS_PER_ITER)]
        run(sum((full_block(blk) for blk in blocks), []), full_block(ATT_BLOCKS_PER_ITER * (j + 1)))
        return carry

    lax.fori_loop(n_skip, n_iters, body, 0)

    outs = []
    for h in range(2):
        a = acc_sc[h]
        outs.append(a[0:HEAD_DIM, :] / a[HEAD_DIM:HEAD_DIM + 1, :])
    o_ref[...] = jnp.concatenate(outs, axis=0).T.astype(o_ref.dtype)


def _attention(qa, ka, vt, stats):
    b, s, _ = qa.shape
    tq, tk = T_Q, T_K
    return pl.pallas_call(
        _attn_kernel,
        grid=(b, N_PAIRS, s // tq),
        in_specs=[pl.BlockSpec((None, tq, PAIR_WIDTH), lambda bi, p, i: (bi, i, p)),
                  pl.BlockSpec((None, s, PAIR_WIDTH), lambda bi, p, i: (bi, 0, p)),
                  pl.BlockSpec((None, 2, s // tk, VT_ROWS, tk), lambda bi, p, i: (bi, p, 0, 0, 0)),
                  pl.BlockSpec((None,) + stats.shape[1:], lambda bi, p, i: (bi, 0, 0, 0))],
        out_specs=pl.BlockSpec((None, tq, LANES), lambda bi, p, i: (bi, i, p)),
        out_shape=jax.ShapeDtypeStruct((b, s, ATTN_WIDTH), _BF16),
        scratch_shapes=([pltpu.VMEM((2, tq, PAIR_WIDTH), _BF16),
                         pltpu.VMEM((2, 1, tq), _F32),
                         pltpu.VMEM((2, VT_ROWS, tq), _F32)]
                        + [pltpu.VMEM((tk, MXU_DIM), _F32) for _ in range(ATT_RING)]
                        + [pltpu.VMEM((1, MXU_DIM), _F32) for _ in range(ATT_RING)]),
        compiler_params=pltpu.CompilerParams(
            dimension_semantics=("arbitrary", "arbitrary", "arbitrary"),
            vmem_limit_bytes=VMEM_LIMIT),
        name="fox_attn",
    )(qa, ka, vt, stats)


def _layer_norm(z, g, b):
    mu = jnp.mean(z, axis=-1, keepdims=True)
    zc = z - mu
    var = jnp.mean(zc * zc, axis=-1, keepdims=True)
    return zc * lax.rsqrt(var + LN_EPS) * g + b


def _post_kernel(x_ref, pool_ref, attn_ref, wop_ref, woa_ref, g1_ref, b1_ref,
                 w1_ref, w2_ref, g2_ref, b2_ref, o_ref):
    sub = x_ref.shape[0] // POST_SUBTILES
    rows = [slice(n * sub, (n + 1) * sub) for n in range(POST_SUBTILES)]

    def out_proj(r):
        return _dot(pool_ref[r, :], wop_ref[...]) + _dot(attn_ref[r, :], woa_ref[...])

    def mlp_chunks(x1b, y, chunks):
        for c in chunks:
            sl = slice(c * FF_CHUNK, (c + 1) * FF_CHUNK)
            hid = jnp.maximum(_dot(x1b, w1_ref[:, sl]), 0.0)
            y = y + _dot((hid * hid).astype(_BF16), w2_ref[sl, :])
        return y

    def ln1(r, mix):
        return _layer_norm(ALPHA * x_ref[r, :] + mix, g1_ref[...], b1_ref[...])

    def ln2_store(r, x1, y):
        o_ref[r, :] = _layer_norm(ALPHA * x1 + y, g2_ref[...], b2_ref[...])

    first, rest = [0], list(range(1, D_FF // FF_CHUNK))
    mix = [out_proj(r) for r in rows]
    x1 = [ln1(rows[0], mix[0])] + [None] * (len(rows) - 1)
    y_prev = None
    for n, r in enumerate(rows):
        x1n_b = x1[n].astype(_BF16)
        y = mlp_chunks(x1n_b, jnp.zeros_like(x1[n]), first)
        if n + 1 < len(rows):
            x1[n + 1] = ln1(rows[n + 1], mix[n + 1])
        if n > 0:
            ln2_store(rows[n - 1], x1[n - 1], y_prev)
        y_prev = mlp_chunks(x1n_b, y, rest)
    ln2_store(rows[-1], x1[-1], y_prev)


def _post(x, pool, attn, wop, woa, g1, b1, w1, w2, g2, b2):
    t, d = x.shape
    tm = TM_POST
    const = lambda a: pl.BlockSpec(a.shape, lambda i: (0,) * a.ndim, pipeline_mode=pl.Buffered(1))
    tile = lambda w: pl.BlockSpec((tm, w), lambda i: (i, 0))
    return pl.pallas_call(
        _post_kernel,
        grid=(t // tm,),
        in_specs=[tile(d), tile(512), tile(512), const(wop), const(woa), const(g1), const(b1),
                  const(w1), const(w2), const(g2), const(b2)],
        out_specs=tile(d),
        out_shape=jax.ShapeDtypeStruct((t, d), _F32),
        compiler_params=pltpu.CompilerParams(
            dimension_semantics=("arbitrary",), vmem_limit_bytes=VMEM_LIMIT),
        name="post",
    )(x, pool, attn, wop, woa, g1, b1, w1, w2, g2, b2)


def _gate_placement():
    eq = np.zeros((LANES, N_PAIRS * LANES), np.float32)
    ek = np.zeros((LANES, N_PAIRS * LANES), np.float32)
    for h in range(N_HEADS):
        base = (h // 2) * LANES + (h % 2) * GATE_COLS_PER_HEAD
        for piece in range(3):
            eq[8 * piece + h, base + piece] = 1.0
            eq[GATE_ONE_LANE, base + 3 + piece] = 1.0
            ek[GATE_ONE_LANE, base + piece] = 1.0
            ek[8 * piece + h, base + 3 + piece] = -1.0
    return jnp.asarray(eq, _BF16), jnp.asarray(ek, _BF16)


def kernel(x, w_in, b_f, pool_w, pool_scale, w_out, ln1_g, ln1_b, w_mlp1, w_mlp2, ln2_g, ln2_b):
    b, s, d = x.shape
    assert s % T_Q == 0 and (b * s) % TM_POST == 0
    eq, ek = _gate_placement()
    hind = jnp.asarray(np.arange(ATTN_WIDTH)[:, None] // HEAD_DIM == np.arange(LANES)[None, :], _BF16)
    o_q = POOL_WIDTH
    o_k = o_q + ATTN_WIDTH
    o_v = o_k + ATTN_WIDTH
    o_f = o_v + ATTN_WIDTH
    row = lambda a: a.reshape(1, -1)
    for l in range(DEPTH):
        w = w_in[l]
        wu = w[:, :o_q].astype(_BF16)
        wq = (w[:, o_q:o_k] * (HEAD_DIM ** -0.5 * LOG2E)).astype(_BF16)
        wk = w[:, o_k:o_v].astype(_BF16)
        wvt = w[:, o_v:o_f].T.astype(_BF16)
        wf = jnp.pad(jnp.tile(w[:, o_f:], (1, GATE_COPIES)),
                     ((0, 0), (0, LANES - GATE_COPIES * N_HEADS))).astype(_BF16)
        bf = jnp.pad(jnp.tile(b_f[l], GATE_COPIES), (0, LANES - GATE_COPIES * N_HEADS)).reshape(1, LANES)
        pool, qa, ka, vt, stats = _inproj(x, wu, wq, wk, wvt, wf, bf, pool_w[l].astype(_BF16),
                                          row(pool_scale[l]), eq, ek, hind)
        attn = _attention(qa, ka, vt, stats.transpose(0, 2, 1, 3))
        wo = w_out[l].astype(_BF16)
        x = _post(x.reshape(b * s, d), pool.reshape(b * s, -1), attn.reshape(b * s, -1),
                  wo[:POOL_WIDTH], wo[POOL_WIDTH:], row(ln1_g[l]), row(ln1_b[l]),
                  w_mlp1[l].astype(_BF16), w_mlp2[l].astype(_BF16),
                  row(ln2_g[l]), row(ln2_b[l])).reshape(b, s, d)
    return x
```

```python
import jax
import jax.numpy as jnp
import numpy as np
from jax import lax
from jax.experimental import pallas as pl
from jax.experimental.pallas import tpu as pltpu

D_MODEL = 1024
DEPTH = 2
POOL_WIDTH = 512
POOL_WINDOWS = (2, 4, 8, 16)
POOL_GROUP_WIDTH = 128
ATTN_WIDTH = 512
HEAD_DIM = 64
N_HEADS = 8
N_PAIRS = N_HEADS // 2
D_FF = 4 * D_MODEL
LN_EPS = 1e-5
NEG_INF = -1e30
ALPHA = float((2.0 * DEPTH) ** 0.25)
LOG2E = 1.4426950408889634

LANES = 128
MXU_DIM = 256
POOL_HALO = 32
GATE_ONE_LANE = 24
GATE_COPIES = 9
GATE_COLS_PER_HEAD = 6
PAIR_WIDTH = 2 * LANES
VT_ROWS = 80
STAT_Q2, STAT_K2, STAT_C_FIRST, STAT_C_LAST, STAT_ROWS = 0, 1, 2, 3, 4
PRUNE_MARGIN = 152.0
PRUNE_NORM_SAFETY = 1.02

T_Q = 2048
T_K = 512
ATT_CW = MXU_DIM
ATT_BLOCKS_PER_ITER = 1
ATT_RING = 4
ATT_LOOKAHEAD = 3
TM_IN = T_K
TM_POST = 1024
POST_SUBTILES = 4
FF_CHUNK = 1024
VMEM_LIMIT = 56 * 1024 * 1024

_BF16 = jnp.bfloat16
_F32 = jnp.float32
_NT = (((1,), (1,)), ((), ()))


def _dot(a, b):
    return jnp.dot(a, b, preferred_element_type=_F32)


def _dot_nt(a, b):
    return lax.dot_general(a, b, _NT, preferred_element_type=_F32)


def _split3(x):
    hi = x.astype(_BF16)
    r = x - hi.astype(_F32)
    mid = r.astype(_BF16)
    lo = (r - mid.astype(_F32)).astype(_BF16)
    return hi, mid, lo


def _inproj_kernel(x_ref, wu_ref, wq_ref, wk_ref, wvt_ref, wf_ref, bf_ref, pw_ref, ps_ref,
                   eq_ref, ek_ref, hind_ref,
                   pool_ref, qa_ref, ka_ref, vt_ref, st_ref,
                   ubuf, sb1, sb2, sb3, carry):
    i = pl.program_id(1)
    tm = x_ref.shape[0]
    hl = POOL_HALO

    @pl.when(i == 0)
    def _():
        ubuf[0:hl, :] = jnp.zeros((hl, POOL_WIDTH), _F32)
        carry[...] = jnp.zeros(carry.shape, _F32)

    @pl.when(i > 0)
    def _():
        ubuf[0:hl, :] = ubuf[tm:tm + hl, :]

    xb = x_ref[...].astype(_BF16)

    lane = lax.broadcasted_iota(jnp.int32, (tm, LANES), 1)
    ng = GATE_ONE_LANE
    f = _dot(xb, wf_ref[...]) + bf_ref[...]
    ubuf[hl:hl + tm, :] = _dot(xb, wu_ref[...])

    lf = (jnp.minimum(f, 0.0) - jnp.log1p(jnp.exp(-jnp.abs(f)))) * LOG2E
    l_hi, l_mid, l_lo = _split3(lf)
    zero = jnp.zeros_like(l_hi)
    pieces = jnp.where(lane < ng, l_hi,
                       jnp.where(lane < 2 * ng, l_mid, jnp.where(lane < 3 * ng, l_lo, zero)))
    row = lax.broadcasted_iota(jnp.int32, (tm, tm), 0)
    col = lax.broadcasted_iota(jnp.int32, (tm, tm), 1)
    tri = jnp.where(col <= row, 1.0, 0.0).astype(_BF16)
    cs = _dot(tri, pieces)
    qf = _dot(xb, wq_ref[...])
    q = qf.astype(_BF16)

    s2 = ubuf[8:tm + 32, :] + ubuf[7:tm + 31, :]
    sb1[8:tm + 32, :] = s2[:, 128:512]
    s4 = sb1[16:tm + 32, :] + sb1[14:tm + 30, :]
    sb2[16:tm + 32, :] = s4[:, 128:384]
    s8 = sb2[24:tm + 32, :] + sb2[20:tm + 28, :]
    sb3[24:tm + 32, :] = s8[:, 128:256]
    s16 = sb3[32:tm + 32, :] + sb3[24:tm + 24, :]
    sums = (s2[24:, 0:128], s4[16:, 0:128], s8[8:, 0:128], s16)

    c = cs + pltpu.roll(cs, LANES - ng, axis=1) + pltpu.roll(cs, LANES - 2 * ng, axis=1)
    c = jnp.where(lane < ng, c, 0.0) + carry[7:8, :]
    carry[...] = c[tm - 8:tm, :]
    c_hi, c_mid, c_lo = _split3(c)
    one = jnp.where(lane == GATE_ONE_LANE, 1.0, 0.0).astype(_BF16)
    gate = jnp.where(lane < 8, c_hi, jnp.where(lane < 16, c_mid, jnp.where(lane < 24, c_lo, one)))
    gq = _dot(gate, eq_ref[...]).astype(_BF16)
    gk = _dot(gate, ek_ref[...]).astype(_BF16)
    kf = _dot(xb, wk_ref[...])
    k = kf.astype(_BF16)

    st_ref[STAT_Q2:STAT_Q2 + 1, :] = jnp.max(
        _dot((qf * qf).astype(_BF16), hind_ref[...]), axis=0, keepdims=True)
    st_ref[STAT_K2:STAT_K2 + 1, :] = jnp.max(
        _dot((kf * kf).astype(_BF16), hind_ref[...]), axis=0, keepdims=True)
    st_ref[STAT_C_FIRST:STAT_C_FIRST + 1, :] = c[0:1, :]
    st_ref[STAT_C_LAST:STAT_C_LAST + 1, :] = c[tm - 1:tm, :]
    st_ref[STAT_ROWS:8, :] = jnp.zeros((8 - STAT_ROWS, LANES), _F32)

    pos = (i * tm + lax.broadcasted_iota(jnp.int32, (tm, 1), 0) + 1).astype(_F32)
    for g, win in enumerate(POOL_WINDOWS):
        sl = slice(g * POOL_GROUP_WIDTH, (g + 1) * POOL_GROUP_WIDTH)
        cnt = jnp.minimum(pos, float(win))
        d = sums[g] / cnt - ubuf[hl:hl + tm, sl]
        y = _dot(d.astype(_BF16), pw_ref[g]) * ps_ref[:, sl]
        pool_ref[:, sl] = y.astype(_BF16)

    vt = _dot_nt(wvt_ref[...], xb)
    pad_rows = VT_ROWS - HEAD_DIM
    ones_row = jnp.where(lax.broadcasted_iota(jnp.int32, (pad_rows, tm), 0) == 0, 1.0, 0.0)
    for h in range(N_HEADS):
        vt_ref[h, 0:HEAD_DIM, :] = vt[h * HEAD_DIM:(h + 1) * HEAD_DIM, :].astype(_BF16)
        vt_ref[h, HEAD_DIM:VT_ROWS, :] = ones_row.astype(_BF16)

    for p in range(N_PAIRS):
        src = slice(p * LANES, (p + 1) * LANES)
        qa_ref[:, p * PAIR_WIDTH:p * PAIR_WIDTH + LANES] = q[:, src]
        qa_ref[:, p * PAIR_WIDTH + LANES:(p + 1) * PAIR_WIDTH] = gq[:, src]
        ka_ref[:, p * PAIR_WIDTH:p * PAIR_WIDTH + LANES] = k[:, src]
        ka_ref[:, p * PAIR_WIDTH + LANES:(p + 1) * PAIR_WIDTH] = gk[:, src]


def _inproj(x, wu, wq, wk, wvt, wf, bf, pw, ps, eq, ek, hind):
    b, s, d = x.shape
    tm = TM_IN
    const = lambda shape: pl.BlockSpec(shape, lambda bi, i: (0,) * len(shape))
    tile = lambda w: pl.BlockSpec((None, tm, w), lambda bi, i: (bi, i, 0))
    out_shape = [jax.ShapeDtypeStruct((b, s, POOL_WIDTH), _BF16),
                 jax.ShapeDtypeStruct((b, s, N_PAIRS * PAIR_WIDTH), _BF16),
                 jax.ShapeDtypeStruct((b, s, N_PAIRS * PAIR_WIDTH), _BF16),
                 jax.ShapeDtypeStruct((b, N_HEADS, s // tm, VT_ROWS, tm), _BF16),
                 jax.ShapeDtypeStruct((b, s // tm, 8, LANES), _F32)]
    out_specs = [tile(POOL_WIDTH), tile(N_PAIRS * PAIR_WIDTH), tile(N_PAIRS * PAIR_WIDTH),
                 pl.BlockSpec((None, N_HEADS, None, VT_ROWS, tm), lambda bi, i: (bi, 0, i, 0, 0)),
                 pl.BlockSpec((None, None, 8, LANES), lambda bi, i: (bi, i, 0, 0))]
    return pl.pallas_call(
        _inproj_kernel,
        grid=(b, s // tm),
        in_specs=[tile(d), const(wu.shape), const(wq.shape), const(wk.shape), const(wvt.shape),
                  const(wf.shape), const(bf.shape), const(pw.shape), const(ps.shape),
                  const(eq.shape), const(ek.shape), const(hind.shape)],
        out_specs=out_specs,
        out_shape=out_shape,
        scratch_shapes=[pltpu.VMEM((tm + POOL_HALO, 512), _F32),
                        pltpu.VMEM((tm + POOL_HALO, 384), _F32),
                        pltpu.VMEM((tm + POOL_HALO, 256), _F32),
                        pltpu.VMEM((tm + POOL_HALO, 128), _F32),
                        pltpu.VMEM((8, LANES), _F32)],
        compiler_params=pltpu.CompilerParams(
            dimension_semantics=("arbitrary", "arbitrary"), vmem_limit_bytes=VMEM_LIMIT),
        name="inproj",
    )(x, wu, wq, wk, wvt, wf, bf, pw, ps, eq, ek, hind)


def _attn_kernel(qa_ref, ka_ref, vt_ref, st_ref, o_ref, qh_sc, m_sc, acc_sc, *rings):
    s_ring, mx_ring = rings[:ATT_RING], rings[ATT_RING:]
    i = pl.program_id(2)
    tq, tk, cw = T_Q, T_K, ATT_CW
    g = GATE_COLS_PER_HEAD
    lane = lax.broadcasted_iota(jnp.int32, (tq, PAIR_WIDTH), 1)
    qf = qa_ref[...].astype(_F32)
    keep0 = (lane < HEAD_DIM) | ((lane >= LANES) & (lane < LANES + g))
    keep1 = ((lane >= HEAD_DIM) & (lane < LANES)) | ((lane >= LANES + g) & (lane < LANES + 2 * g))
    qh_sc[0] = jnp.where(keep0, qf, 0.0).astype(_BF16)
    qh_sc[1] = jnp.where(keep1, qf, 0.0).astype(_BF16)
    m_sc[...] = jnp.full(m_sc.shape, NEG_INF, _F32)
    acc_sc[...] = jnp.zeros(acc_sc.shape, _F32)

    def produce(slot, blk, h, c, mask_off, rows):
        kb = ka_ref[pl.ds(pl.multiple_of(blk * tk, tk), rows), :]
        s = _dot_nt(kb, qh_sc[h, c * cw:(c + 1) * cw, :])
        if mask_off is not None:
            kpos = lax.broadcasted_iota(jnp.int32, s.shape, 0) + mask_off
            qpos = lax.broadcasted_iota(jnp.int32, s.shape, 1) + c * cw
            s = jnp.where(kpos <= qpos, s, NEG_INF)
        s_ring[slot][0:rows, :] = s
        mx_ring[slot][...] = jnp.max(s, axis=0, keepdims=True)

    def consume(slot, blk, h, c, mask_off, rows):
        cs = slice(c * cw, (c + 1) * cw)
        m_prev = m_sc[h, :, cs]
        m_new = jnp.maximum(m_prev, mx_ring[slot][...])
        alpha = jnp.exp2(m_prev - m_new)
        p = jnp.exp2(s_ring[slot][0:rows, :] - m_new).astype(_BF16)
        acc_sc[h, :, cs] = alpha * acc_sc[h, :, cs] + _dot(vt_ref[h, blk, :, 0:rows], p)
        m_sc[h, :, cs] = m_new

    def run(entries, following, pos0=0):
        stream = entries + following[:ATT_LOOKAHEAD]
        for n, unit in enumerate(entries):
            produce((pos0 + n + ATT_LOOKAHEAD) % ATT_RING, *stream[n + ATT_LOOKAHEAD])
            consume((pos0 + n) % ATT_RING, *unit)

    nc = tq // cw
    per_q = tq // tk
    cpb = tk // cw
    units = [(h, c) for c in range(nc) for h in range(2)]
    full_block = lambda blk: [(blk, h, c, None, tk) for h, c in units]
    diag = []
    for r in reversed(range(per_q)):
        for h, c in units:
            if c >= (r + 1) * cpb:
                diag.append((per_q * i + r, h, c, None, tk))
            elif c >= r * cpb:
                diag.append((per_q * i + r, h, c, r * tk, (c - r * cpb + 1) * cw))
    assert len(units) % ATT_RING == 0 and len(diag) % ATT_RING == 0 and ATT_LOOKAHEAD < ATT_RING
    assert per_q % ATT_BLOCKS_PER_ITER == 0

    split = len(diag) - 2 * ATT_LOOKAHEAD
    assert {(h, c) for _, h, c, _, _ in diag[:split]} == set(units)
    for n in range(ATT_LOOKAHEAD):
        produce(n, *diag[n])
    run(diag[:split], diag[split:], 0)

    n_iters = (per_q // ATT_BLOCKS_PER_ITER) * i
    tiles_per_q = tq // TM_IN
    tiles_per_iter = ATT_BLOCKS_PER_ITER * tk // TM_IN
    n_tiles = st_ref.shape[1]
    trow = lax.broadcasted_iota(jnp.int32, (n_tiles, LANES), 0)
    in_q = (trow >= i * tiles_per_q) & (trow < (i + 1) * tiles_per_q)
    q2 = jnp.max(jnp.where(in_q, st_ref[STAT_Q2], 0.0), axis=0, keepdims=True)
    k2 = jnp.max(st_ref[STAT_K2], axis=0, keepdims=True)
    bound = PRUNE_NORM_SAFETY * jnp.sqrt(q2 * k2)
    c_t0 = jnp.max(jnp.where(trow == i * tiles_per_q, st_ref[STAT_C_FIRST], -jnp.inf),
                   axis=0, keepdims=True)
    m_min = jnp.min(m_sc[...])
    prunable = (c_t0 - st_ref[STAT_C_LAST]) + bound <= m_min - PRUNE_MARGIN
    group_end = (trow + 1) % tiles_per_iter == 0
    count = jnp.sum(jnp.where(prunable & group_end, 1.0, 0.0), axis=0, keepdims=True)
    head_lane = lax.broadcasted_iota(jnp.int32, (1, LANES), 1) - 2 * pl.program_id(1)
    n_skip = jnp.minimum(jnp.sum(jnp.where(head_lane == 0, count, 0.0)),
                         jnp.sum(jnp.where(head_lane == 1, count, 0.0))).astype(jnp.int32)
    n_skip = jnp.minimum(n_skip, n_iters)

    run(diag[split:], full_block(ATT_BLOCKS_PER_ITER * n_skip), split)

    def body(j, carry):
        blocks = [ATT_BLOCKS_PER_ITER * j + r for r in range(ATT_BLOCKS_PER_ITER)]
        run(sum((full_block(blk) for blk in blocks), []), full_block(ATT_BLOCKS_PER_ITER * (j + 1)))
        return carry

    lax.fori_loop(n_skip, n_iters, body, 0)

    outs = []
    for h in range(2):
        a = acc_sc[h]
        outs.append(a[0:HEAD_DIM, :] / a[HEAD_DIM:HEAD_DIM + 1, :])
    o_ref[...] = jnp.concatenate(outs, axis=0).T.astype(o_ref.dtype)


def _attention(qa, ka, vt, stats):
    b, s, _ = qa.shape
    tq, tk = T_Q, T_K
    return pl.pallas_call(
        _attn_kernel,
        grid=(b, N_PAIRS, s // tq),
        in_specs=[pl.BlockSpec((None, tq, PAIR_WIDTH), lambda bi, p, i: (bi, i, p)),
                  pl.BlockSpec((None, s, PAIR_WIDTH), lambda bi, p, i: (bi, 0, p)),
                  pl.BlockSpec((None, 2, s // tk, VT_ROWS, tk), lambda bi, p, i: (bi, p, 0, 0, 0)),
                  pl.BlockSpec((None,) + stats.shape[1:], lambda bi, p, i: (bi, 0, 0, 0))],
        out_specs=pl.BlockSpec((None, tq, LANES), lambda bi, p, i: (bi, i, p)),
        out_shape=jax.ShapeDtypeStruct((b, s, ATTN_WIDTH), _BF16),
        scratch_shapes=([pltpu.VMEM((2, tq, PAIR_WIDTH), _BF16),
                         pltpu.VMEM((2, 1, tq), _F32),
                         pltpu.VMEM((2, VT_ROWS, tq), _F32)]
                        + [pltpu.VMEM((tk, ATT_CW), _F32) for _ in range(ATT_RING)]
                        + [pltpu.VMEM((1, ATT_CW), _F32) for _ in range(ATT_RING)]),
        compiler_params=pltpu.CompilerParams(
            dimension_semantics=("arbitrary", "arbitrary", "arbitrary"),
            vmem_limit_bytes=VMEM_LIMIT),
        name="fox_attn",
    )(qa, ka, vt, stats)


def _layer_norm(z, g, b):
    mu = jnp.mean(z, axis=-1, keepdims=True)
    zc = z - mu
    var = jnp.mean(zc * zc, axis=-1, keepdims=True)
    return zc * lax.rsqrt(var + LN_EPS) * g + b


def _post_kernel(x_ref, pool_ref, attn_ref, wop_ref, woa_ref, g1_ref, b1_ref,
                 w1_ref, w2_ref, g2_ref, b2_ref, o_ref):
    sub = x_ref.shape[0] // POST_SUBTILES
    rows = [slice(n * sub, (n + 1) * sub) for n in range(POST_SUBTILES)]

    def out_proj(r):
        return _dot(pool_ref[r, :], wop_ref[...]) + _dot(attn_ref[r, :], woa_ref[...])

    def mlp_chunks(x1b, y, chunks):
        for c in chunks:
            sl = slice(c * FF_CHUNK, (c + 1) * FF_CHUNK)
            hid = jnp.maximum(_dot(x1b, w1_ref[:, sl]), 0.0)
            y = y + _dot((hid * hid).astype(_BF16), w2_ref[sl, :])
        return y

    def ln1(r, mix):
        return _layer_norm(ALPHA * x_ref[r, :] + mix, g1_ref[...], b1_ref[...])

    def ln2_store(r, x1, y):
        o_ref[r, :] = _layer_norm(ALPHA * x1 + y, g2_ref[...], b2_ref[...])

    first, rest = [0], list(range(1, D_FF // FF_CHUNK))
    mix = [out_proj(r) for r in rows]
    x1 = [ln1(rows[0], mix[0])] + [None] * (len(rows) - 1)
    y_prev = None
    for n, r in enumerate(rows):
        x1n_b = x1[n].astype(_BF16)
        y = mlp_chunks(x1n_b, jnp.zeros_like(x1[n]), first)
        if n + 1 < len(rows):
            x1[n + 1] = ln1(rows[n + 1], mix[n + 1])
        if n > 0:
            ln2_store(rows[n - 1], x1[n - 1], y_prev)
        y_prev = mlp_chunks(x1n_b, y, rest)
    ln2_store(rows[-1], x1[-1], y_prev)


def _post(x, pool, attn, wop, woa, g1, b1, w1, w2, g2, b2):
    t, d = x.shape
    tm = TM_POST
    const = lambda a: pl.BlockSpec(a.shape, lambda i: (0,) * a.ndim, pipeline_mode=pl.Buffered(1))
    tile = lambda w: pl.BlockSpec((tm, w), lambda i: (i, 0))
    return pl.pallas_call(
        _post_kernel,
        grid=(t // tm,),
        in_specs=[tile(d), tile(512), tile(512), const(wop), const(woa), const(g1), const(b1),
                  const(w1), const(w2), const(g2), const(b2)],
        out_specs=tile(d),
        out_shape=jax.ShapeDtypeStruct((t, d), _F32),
        compiler_params=pltpu.CompilerParams(
            dimension_semantics=("arbitrary",), vmem_limit_bytes=VMEM_LIMIT),
        name="post",
    )(x, pool, attn, wop, woa, g1, b1, w1, w2, g2, b2)


def _gate_placement():
    eq = np.zeros((LANES, N_PAIRS * LANES), np.float32)
    ek = np.zeros((LANES, N_PAIRS * LANES), np.float32)
    for h in range(N_HEADS):
        base = (h // 2) * LANES + (h % 2) * GATE_COLS_PER_HEAD
        for piece in range(3):
            eq[8 * piece + h, base + piece] = 1.0
            eq[GATE_ONE_LANE, base + 3 + piece] = 1.0
            ek[GATE_ONE_LANE, base + piece] = 1.0
            ek[8 * piece + h, base + 3 + piece] = -1.0
    return jnp.asarray(eq, _BF16), jnp.asarray(ek, _BF16)


def kernel(x, w_in, b_f, pool_w, pool_scale, w_out, ln1_g, ln1_b, w_mlp1, w_mlp2, ln2_g, ln2_b):
    b, s, d = x.shape
    assert s % T_Q == 0 and (b * s) % TM_POST == 0
    eq, ek = _gate_placement()
    hind = jnp.asarray(np.arange(ATTN_WIDTH)[:, None] // HEAD_DIM == np.arange(LANES)[None, :], _BF16)
    o_q = POOL_WIDTH
    o_k = o_q + ATTN_WIDTH
    o_v = o_k + ATTN_WIDTH
    o_f = o_v + ATTN_WIDTH
    row = lambda a: a.reshape(1, -1)
    for l in range(DEPTH):
        w = w_in[l]
        wu = w[:, :o_q].astype(_BF16)
        wq = (w[:, o_q:o_k] * (HEAD_DIM ** -0.5 * LOG2E)).astype(_BF16)
        wk = w[:, o_k:o_v].astype(_BF16)
        wvt = w[:, o_v:o_f].T.astype(_BF16)
        wf = jnp.pad(jnp.tile(w[:, o_f:], (1, GATE_COPIES)),
                     ((0, 0), (0, LANES - GATE_COPIES * N_HEADS))).astype(_BF16)
        bf = jnp.pad(jnp.tile(b_f[l], GATE_COPIES), (0, LANES - GATE_COPIES * N_HEADS)).reshape(1, LANES)
        pool, qa, ka, vt, stats = _inproj(x, wu, wq, wk, wvt, wf, bf, pool_w[l].astype(_BF16),
                                          row(pool_scale[l]), eq, ek, hind)
        attn = _attention(qa, ka, vt, stats.transpose(0, 2, 1, 3))
        wo = w_out[l].astype(_BF16)
        x = _post(x.reshape(b * s, d), pool.reshape(b * s, -1), attn.reshape(b * s, -1),
                  wo[:POOL_WIDTH], wo[POOL_WIDTH:], row(ln1_g[l]), row(ln1_b[l]),
                  w_mlp1[l].astype(_BF16), w_mlp2[l].astype(_BF16),
                  row(ln2_g[l]), row(ln2_b[l])).reshape(b, s, d)
    return x
```

```python
import jax
import jax.numpy as jnp
import numpy as np
from jax import lax
from jax.experimental import pallas as pl
from jax.experimental.pallas import tpu as pltpu

D_MODEL = 1024
DEPTH = 2
POOL_WIDTH = 512
POOL_WINDOWS = (2, 4, 8, 16)
POOL_GROUP_WIDTH = 128
ATTN_WIDTH = 512
HEAD_DIM = 64
N_HEADS = 8
N_PAIRS = N_HEADS // 2
D_FF = 4 * D_MODEL
LN_EPS = 1e-5
NEG_INF = -1e30
ALPHA = float((2.0 * DEPTH) ** 0.25)
LOG2E = 1.4426950408889634

LANES = 128
MXU_DIM = 256
POOL_HALO = 32
GATE_ONE_LANE = 24
GATE_COPIES = 9
GATE_COLS_PER_HEAD = 6
PAIR_WIDTH = 2 * LANES
VT_ROWS = 80
STAT_Q2, STAT_K2, STAT_C_FIRST, STAT_C_LAST, STAT_ROWS = 0, 1, 2, 3, 4
PRUNE_MARGIN = 152.0
PRUNE_NORM_SAFETY = 1.02

T_Q = 2048
T_K = 512
ATT_CW = MXU_DIM
ATT_BLOCKS_PER_ITER = 1
ATT_RING = 4
ATT_LOOKAHEAD = 3
ATT_RING_PAD_ROWS = 8
TM_IN = T_K
TM_POST = 1024
POST_SUBTILES = 4
FF_CHUNK = 1024
VMEM_LIMIT = 56 * 1024 * 1024

_BF16 = jnp.bfloat16
_F32 = jnp.float32
_NT = (((1,), (1,)), ((), ()))


def _dot(a, b):
    return jnp.dot(a, b, preferred_element_type=_F32)


def _dot_nt(a, b):
    return lax.dot_general(a, b, _NT, preferred_element_type=_F32)


def _split3(x):
    hi = x.astype(_BF16)
    r = x - hi.astype(_F32)
    mid = r.astype(_BF16)
    lo = (r - mid.astype(_F32)).astype(_BF16)
    return hi, mid, lo


def _inproj_kernel(x_ref, wu_ref, wq_ref, wk_ref, wvt_ref, wf_ref, bf_ref, pw_ref, ps_ref,
                   eq_ref, ek_ref, hind_ref,
                   pool_ref, qa_ref, ka_ref, vt_ref, st_ref,
                   ubuf, sb1, sb2, sb3, carry):
    i = pl.program_id(1)
    tm = x_ref.shape[0]
    hl = POOL_HALO

    @pl.when(i == 0)
    def _():
        ubuf[0:hl, :] = jnp.zeros((hl, POOL_WIDTH), _F32)
        carry[...] = jnp.zeros(carry.shape, _F32)

    @pl.when(i > 0)
    def _():
        ubuf[0:hl, :] = ubuf[tm:tm + hl, :]

    xb = x_ref[...].astype(_BF16)

    lane = lax.broadcasted_iota(jnp.int32, (tm, LANES), 1)
    ng = GATE_ONE_LANE
    f = _dot(xb, wf_ref[...]) + bf_ref[...]
    ubuf[hl:hl + tm, :] = _dot(xb, wu_ref[...])

    lf = (jnp.minimum(f, 0.0) - jnp.log1p(jnp.exp(-jnp.abs(f)))) * LOG2E
    l_hi, l_mid, l_lo = _split3(lf)
    zero = jnp.zeros_like(l_hi)
    pieces = jnp.where(lane < ng, l_hi,
                       jnp.where(lane < 2 * ng, l_mid, jnp.where(lane < 3 * ng, l_lo, zero)))
    row = lax.broadcasted_iota(jnp.int32, (tm, tm), 0)
    col = lax.broadcasted_iota(jnp.int32, (tm, tm), 1)
    tri = jnp.where(col <= row, 1.0, 0.0).astype(_BF16)
    cs = _dot(tri, pieces)
    qf = _dot(xb, wq_ref[...])
    q = qf.astype(_BF16)

    s2 = ubuf[8:tm + 32, :] + ubuf[7:tm + 31, :]
    sb1[8:tm + 32, :] = s2[:, 128:512]
    s4 = sb1[16:tm + 32, :] + sb1[14:tm + 30, :]
    sb2[16:tm + 32, :] = s4[:, 128:384]
    s8 = sb2[24:tm + 32, :] + sb2[20:tm + 28, :]
    sb3[24:tm + 32, :] = s8[:, 128:256]
    s16 = sb3[32:tm + 32, :] + sb3[24:tm + 24, :]
    sums = (s2[24:, 0:128], s4[16:, 0:128], s8[8:, 0:128], s16)

    c = cs + pltpu.roll(cs, LANES - ng, axis=1) + pltpu.roll(cs, LANES - 2 * ng, axis=1)
    c = jnp.where(lane < ng, c, 0.0) + carry[7:8, :]
    carry[...] = c[tm - 8:tm, :]
    c_hi, c_mid, c_lo = _split3(c)
    one = jnp.where(lane == GATE_ONE_LANE, 1.0, 0.0).astype(_BF16)
    gate = jnp.where(lane < 8, c_hi, jnp.where(lane < 16, c_mid, jnp.where(lane < 24, c_lo, one)))
    gq = _dot(gate, eq_ref[...]).astype(_BF16)
    gk = _dot(gate, ek_ref[...]).astype(_BF16)
    kf = _dot(xb, wk_ref[...])
    k = kf.astype(_BF16)

    st_ref[STAT_Q2:STAT_Q2 + 1, :] = jnp.max(
        _dot((qf * qf).astype(_BF16), hind_ref[...]), axis=0, keepdims=True)
    st_ref[STAT_K2:STAT_K2 + 1, :] = jnp.max(
        _dot((kf * kf).astype(_BF16), hind_ref[...]), axis=0, keepdims=True)
    st_ref[STAT_C_FIRST:STAT_C_FIRST + 1, :] = c[0:1, :]
    st_ref[STAT_C_LAST:STAT_C_LAST + 1, :] = c[tm - 1:tm, :]
    st_ref[STAT_ROWS:8, :] = jnp.zeros((8 - STAT_ROWS, LANES), _F32)

    pos = (i * tm + lax.broadcasted_iota(jnp.int32, (tm, 1), 0) + 1).astype(_F32)
    for g, win in enumerate(POOL_WINDOWS):
        sl = slice(g * POOL_GROUP_WIDTH, (g + 1) * POOL_GROUP_WIDTH)
        cnt = jnp.minimum(pos, float(win))
        d = sums[g] / cnt - ubuf[hl:hl + tm, sl]
        y = _dot(d.astype(_BF16), pw_ref[g]) * ps_ref[:, sl]
        pool_ref[:, sl] = y.astype(_BF16)

    vt = _dot_nt(wvt_ref[...], xb)
    pad_rows = VT_ROWS - HEAD_DIM
    ones_row = jnp.where(lax.broadcasted_iota(jnp.int32, (pad_rows, tm), 0) == 0, 1.0, 0.0)
    for h in range(N_HEADS):
        vt_ref[h, 0:HEAD_DIM, :] = vt[h * HEAD_DIM:(h + 1) * HEAD_DIM, :].astype(_BF16)
        vt_ref[h, HEAD_DIM:VT_ROWS, :] = ones_row.astype(_BF16)

    for p in range(N_PAIRS):
        src = slice(p * LANES, (p + 1) * LANES)
        qa_ref[:, p * PAIR_WIDTH:p * PAIR_WIDTH + LANES] = q[:, src]
        qa_ref[:, p * PAIR_WIDTH + LANES:(p + 1) * PAIR_WIDTH] = gq[:, src]
        ka_ref[:, p * PAIR_WIDTH:p * PAIR_WIDTH + LANES] = k[:, src]
        ka_ref[:, p * PAIR_WIDTH + LANES:(p + 1) * PAIR_WIDTH] = gk[:, src]


def _inproj(x, wu, wq, wk, wvt, wf, bf, pw, ps, eq, ek, hind):
    b, s, d = x.shape
    tm = TM_IN
    const = lambda shape: pl.BlockSpec(shape, lambda bi, i: (0,) * len(shape))
    tile = lambda w: pl.BlockSpec((None, tm, w), lambda bi, i: (bi, i, 0))
    out_shape = [jax.ShapeDtypeStruct((b, s, POOL_WIDTH), _BF16),
                 jax.ShapeDtypeStruct((b, s, N_PAIRS * PAIR_WIDTH), _BF16),
                 jax.ShapeDtypeStruct((b, s, N_PAIRS * PAIR_WIDTH), _BF16),
                 jax.ShapeDtypeStruct((b, N_HEADS, s // tm, VT_ROWS, tm), _BF16),
                 jax.ShapeDtypeStruct((b, s // tm, 8, LANES), _F32)]
    out_specs = [tile(POOL_WIDTH), tile(N_PAIRS * PAIR_WIDTH), tile(N_PAIRS * PAIR_WIDTH),
                 pl.BlockSpec((None, N_HEADS, None, VT_ROWS, tm), lambda bi, i: (bi, 0, i, 0, 0)),
                 pl.BlockSpec((None, None, 8, LANES), lambda bi, i: (bi, i, 0, 0))]
    return pl.pallas_call(
        _inproj_kernel,
        grid=(b, s // tm),
        in_specs=[tile(d), const(wu.shape), const(wq.shape), const(wk.shape), const(wvt.shape),
                  const(wf.shape), const(bf.shape), const(pw.shape), const(ps.shape),
                  const(eq.shape), const(ek.shape), const(hind.shape)],
        out_specs=out_specs,
        out_shape=out_shape,
        scratch_shapes=[pltpu.VMEM((tm + POOL_HALO, 512), _F32),
                        pltpu.VMEM((tm + POOL_HALO, 384), _F32),
                        pltpu.VMEM((tm + POOL_HALO, 256), _F32),
                        pltpu.VMEM((tm + POOL_HALO, 128), _F32),
                        pltpu.VMEM((8, LANES), _F32)],
        compiler_params=pltpu.CompilerParams(
            dimension_semantics=("arbitrary", "arbitrary"), vmem_limit_bytes=VMEM_LIMIT),
        name="inproj",
    )(x, wu, wq, wk, wvt, wf, bf, pw, ps, eq, ek, hind)


def _attn_kernel(qa_ref, ka_ref, vt_ref, st_ref, o_ref, qh_sc, m_sc, acc_sc, *rings):
    s_ring, mx_ring = rings[:ATT_RING], rings[ATT_RING:]
    i = pl.program_id(2)
    tq, tk, cw = T_Q, T_K, ATT_CW
    g = GATE_COLS_PER_HEAD
    lane = lax.broadcasted_iota(jnp.int32, (tq, PAIR_WIDTH), 1)
    qf = qa_ref[...].astype(_F32)
    keep0 = (lane < HEAD_DIM) | ((lane >= LANES) & (lane < LANES + g))
    keep1 = ((lane >= HEAD_DIM) & (lane < LANES)) | ((lane >= LANES + g) & (lane < LANES + 2 * g))
    qh_sc[0] = jnp.where(keep0, qf, 0.0).astype(_BF16)
    qh_sc[1] = jnp.where(keep1, qf, 0.0).astype(_BF16)
    m_sc[...] = jnp.full(m_sc.shape, NEG_INF, _F32)
    acc_sc[...] = jnp.zeros(acc_sc.shape, _F32)

    def produce(slot, blk, h, c, mask_off, rows):
        kb = ka_ref[pl.ds(pl.multiple_of(blk * tk, tk), rows), :]
        s = _dot_nt(kb, qh_sc[h, c * cw:(c + 1) * cw, :])
        if mask_off is not None:
            kpos = lax.broadcasted_iota(jnp.int32, s.shape, 0) + mask_off
            qpos = lax.broadcasted_iota(jnp.int32, s.shape, 1) + c * cw
            s = jnp.where(kpos <= qpos, s, NEG_INF)
        s_ring[slot][0:rows, :] = s
        mx_ring[slot][...] = jnp.max(s, axis=0, keepdims=True)

    def consume(slot, blk, h, c, mask_off, rows):
        cs = slice(c * cw, (c + 1) * cw)
        m_prev = m_sc[h, :, cs]
        m_new = jnp.maximum(m_prev, mx_ring[slot][...])
        alpha = jnp.exp2(m_prev - m_new)
        p = jnp.exp2(s_ring[slot][0:rows, :] - m_new).astype(_BF16)
        acc_sc[h, :, cs] = alpha * acc_sc[h, :, cs] + _dot(vt_ref[h, blk, :, 0:rows], p)
        m_sc[h, :, cs] = m_new

    def run(entries, following, pos0=0):
        stream = entries + following[:ATT_LOOKAHEAD]
        for n, unit in enumerate(entries):
            produce((pos0 + n + ATT_LOOKAHEAD) % ATT_RING, *stream[n + ATT_LOOKAHEAD])
            consume((pos0 + n) % ATT_RING, *unit)

    nc = tq // cw
    per_q = tq // tk
    cpb = tk // cw
    units = [(h, c) for c in range(nc) for h in range(2)]
    full_block = lambda blk: [(blk, h, c, None, tk) for h, c in units]
    diag = []
    for r in reversed(range(per_q)):
        for h, c in units:
            if c >= (r + 1) * cpb:
                diag.append((per_q * i + r, h, c, None, tk))
            elif c >= r * cpb:
                diag.append((per_q * i + r, h, c, r * tk, (c - r * cpb + 1) * cw))
    assert len(units) % ATT_RING == 0 and len(diag) % ATT_RING == 0 and ATT_LOOKAHEAD < ATT_RING
    assert per_q % ATT_BLOCKS_PER_ITER == 0

    split = len(diag) - 2 * ATT_LOOKAHEAD
    assert {(h, c) for _, h, c, _, _ in diag[:split]} == set(units)
    for n in range(ATT_LOOKAHEAD):
        produce(n, *diag[n])
    run(diag[:split], diag[split:], 0)

    n_iters = (per_q // ATT_BLOCKS_PER_ITER) * i
    tiles_per_q = tq // TM_IN
    tiles_per_iter = ATT_BLOCKS_PER_ITER * tk // TM_IN
    n_tiles = st_ref.shape[1]
    trow = lax.broadcasted_iota(jnp.int32, (n_tiles, LANES), 0)
    in_q = (trow >= i * tiles_per_q) & (trow < (i + 1) * tiles_per_q)
    q2 = jnp.max(jnp.where(in_q, st_ref[STAT_Q2], 0.0), axis=0, keepdims=True)
    k2 = jnp.max(st_ref[STAT_K2], axis=0, keepdims=True)
    bound = PRUNE_NORM_SAFETY * jnp.sqrt(q2 * k2)
    c_t0 = jnp.max(jnp.where(trow == i * tiles_per_q, st_ref[STAT_C_FIRST], -jnp.inf),
                   axis=0, keepdims=True)
    m_min = jnp.min(m_sc[...])
    prunable = (c_t0 - st_ref[STAT_C_LAST]) + bound <= m_min - PRUNE_MARGIN
    group_end = (trow + 1) % tiles_per_iter == 0
    count = jnp.sum(jnp.where(prunable & group_end, 1.0, 0.0), axis=0, keepdims=True)
    head_lane = lax.broadcasted_iota(jnp.int32, (1, LANES), 1) - 2 * pl.program_id(1)
    n_skip = jnp.minimum(jnp.sum(jnp.where(head_lane == 0, count, 0.0)),
                         jnp.sum(jnp.where(head_lane == 1, count, 0.0))).astype(jnp.int32)
    n_skip = jnp.minimum(n_skip, n_iters)

    run(diag[split:], full_block(ATT_BLOCKS_PER_ITER * n_skip), split)

    def body(j, carry):
        blocks = [ATT_BLOCKS_PER_ITER * j + r for r in range(ATT_BLOCKS_PER_ITER)]
        run(sum((full_block(blk) for blk in blocks), []), full_block(ATT_BLOCKS_PER_ITER * (j + 1)))
        return carry

    lax.fori_loop(n_skip, n_iters, body, 0)

    outs = []
    for h in range(2):
        a = acc_sc[h]
        outs.append(a[0:HEAD_DIM, :] / a[HEAD_DIM:HEAD_DIM + 1, :])
    o_ref[...] = jnp.concatenate(outs, axis=0).T.astype(o_ref.dtype)


def _attention(qa, ka, vt, stats):
    b, s, _ = qa.shape
    tq, tk = T_Q, T_K
    return pl.pallas_call(
        _attn_kernel,
        grid=(b, N_PAIRS, s // tq),
        in_specs=[pl.BlockSpec((None, tq, PAIR_WIDTH), lambda bi, p, i: (bi, i, p)),
                  pl.BlockSpec((None, s, PAIR_WIDTH), lambda bi, p, i: (bi, 0, p)),
                  pl.BlockSpec((None, 2, s // tk, VT_ROWS, tk), lambda bi, p, i: (bi, p, 0, 0, 0)),
                  pl.BlockSpec((None,) + stats.shape[1:], lambda bi, p, i: (bi, 0, 0, 0))],
        out_specs=pl.BlockSpec((None, tq, LANES), lambda bi, p, i: (bi, i, p)),
        out_shape=jax.ShapeDtypeStruct((b, s, ATTN_WIDTH), _BF16),
        scratch_shapes=([pltpu.VMEM((2, tq, PAIR_WIDTH), _BF16),
                         pltpu.VMEM((2, 1, tq), _F32),
                         pltpu.VMEM((2, VT_ROWS, tq), _F32)]
                        + [pltpu.VMEM((tk + ATT_RING_PAD_ROWS, ATT_CW), _F32) for _ in range(ATT_RING)]
                        + [pltpu.VMEM((1, ATT_CW), _F32) for _ in range(ATT_RING)]),
        compiler_params=pltpu.CompilerParams(
            dimension_semantics=("arbitrary", "arbitrary", "arbitrary"),
            vmem_limit_bytes=VMEM_LIMIT),
        name="fox_attn",
    )(qa, ka, vt, stats)


def _layer_norm(z, g, b):
    mu = jnp.mean(z, axis=-1, keepdims=True)
    zc = z - mu
    var = jnp.mean(zc * zc, axis=-1, keepdims=True)
    return zc * lax.rsqrt(var + LN_EPS) * g + b


def _post_kernel(x_ref, pool_ref, attn_ref, wop_ref, woa_ref, g1_ref, b1_ref,
                 w1_ref, w2_ref, g2_ref, b2_ref, o_ref):
    sub = x_ref.shape[0] // POST_SUBTILES
    rows = [slice(n * sub, (n + 1) * sub) for n in range(POST_SUBTILES)]

    def out_proj(r):
        return _dot(pool_ref[r, :], wop_ref[...]) + _dot(attn_ref[r, :], woa_ref[...])

    def mlp_chunks(x1b, y, chunks):
        for c in chunks:
            sl = slice(c * FF_CHUNK, (c + 1) * FF_CHUNK)
            hid = jnp.maximum(_dot(x1b, w1_ref[:, sl]), 0.0)
            y = y + _dot((hid * hid).astype(_BF16), w2_ref[sl, :])
        return y

    def ln1(r, mix):
        return _layer_norm(ALPHA * x_ref[r, :] + mix, g1_ref[...], b1_ref[...])

    def ln2_store(r, x1, y):
        o_ref[r, :] = _layer_norm(ALPHA * x1 + y, g2_ref[...], b2_ref[...])

    first, rest = [0], list(range(1, D_FF // FF_CHUNK))
    mix = [out_proj(r) for r in rows]
    x1 = [ln1(rows[0], mix[0])] + [None] * (len(rows) - 1)
    y_prev = None
    for n, r in enumerate(rows):
        x1n_b = x1[n].astype(_BF16)
        y = mlp_chunks(x1n_b, jnp.zeros_like(x1[n]), first)
        if n + 1 < len(rows):
            x1[n + 1] = ln1(rows[n + 1], mix[n + 1])
        if n > 0:
            ln2_store(rows[n - 1], x1[n - 1], y_prev)
        y_prev = mlp_chunks(x1n_b, y, rest)
    ln2_store(rows[-1], x1[-1], y_prev)


def _post(x, pool, attn, wop, woa, g1, b1, w1, w2, g2, b2):
    t, d = x.shape
    tm = TM_POST
    const = lambda a: pl.BlockSpec(a.shape, lambda i: (0,) * a.ndim, pipeline_mode=pl.Buffered(1))
    tile = lambda w: pl.BlockSpec((tm, w), lambda i: (i, 0))
    return pl.pallas_call(
        _post_kernel,
        grid=(t // tm,),
        in_specs=[tile(d), tile(512), tile(512), const(wop), const(woa), const(g1), const(b1),
                  const(w1), const(w2), const(g2), const(b2)],
        out_specs=tile(d),
        out_shape=jax.ShapeDtypeStruct((t, d), _F32),
        compiler_params=pltpu.CompilerParams(
            dimension_semantics=("arbitrary",), vmem_limit_bytes=VMEM_LIMIT),
        name="post",
    )(x, pool, attn, wop, woa, g1, b1, w1, w2, g2, b2)


def _gate_placement():
    eq = np.zeros((LANES, N_PAIRS * LANES), np.float32)
    ek = np.zeros((LANES, N_PAIRS * LANES), np.float32)
    for h in range(N_HEADS):
        base = (h // 2) * LANES + (h % 2) * GATE_COLS_PER_HEAD
        for piece in range(3):
            eq[8 * piece + h, base + piece] = 1.0
            eq[GATE_ONE_LANE, base + 3 + piece] = 1.0
            ek[GATE_ONE_LANE, base + piece] = 1.0
            ek[8 * piece + h, base + 3 + piece] = -1.0
    return jnp.asarray(eq, _BF16), jnp.asarray(ek, _BF16)


def kernel(x, w_in, b_f, pool_w, pool_scale, w_out, ln1_g, ln1_b, w_mlp1, w_mlp2, ln2_g, ln2_b):
    b, s, d = x.shape
    assert s % T_Q == 0 and (b * s) % TM_POST == 0
    eq, ek = _gate_placement()
    hind = jnp.asarray(np.arange(ATTN_WIDTH)[:, None] // HEAD_DIM == np.arange(LANES)[None, :], _BF16)
    o_q = POOL_WIDTH
    o_k = o_q + ATTN_WIDTH
    o_v = o_k + ATTN_WIDTH
    o_f = o_v + ATTN_WIDTH
    row = lambda a: a.reshape(1, -1)
    for l in range(DEPTH):
        w = w_in[l]
        wu = w[:, :o_q].astype(_BF16)
        wq = (w[:, o_q:o_k] * (HEAD_DIM ** -0.5 * LOG2E)).astype(_BF16)
        wk = w[:, o_k:o_v].astype(_BF16)
        wvt = w[:, o_v:o_f].T.astype(_BF16)
        wf = jnp.pad(jnp.tile(w[:, o_f:], (1, GATE_COPIES)),
                     ((0, 0), (0, LANES - GATE_COPIES * N_HEADS))).astype(_BF16)
        bf = jnp.pad(jnp.tile(b_f[l], GATE_COPIES), (0, LANES - GATE_COPIES * N_HEADS)).reshape(1, LANES)
        pool, qa, ka, vt, stats = _inproj(x, wu, wq, wk, wvt, wf, bf, pool_w[l].astype(_BF16),
                                          row(pool_scale[l]), eq, ek, hind)
        attn = _attention(qa, ka, vt, stats.transpose(0, 2, 1, 3))
        wo = w_out[l].astype(_BF16)
        x = _post(x.reshape(b * s, d), pool.reshape(b * s, -1), attn.reshape(b * s, -1),
                  wo[:POOL_WIDTH], wo[POOL_WIDTH:], row(ln1_g[l]), row(ln1_b[l]),
                  w_mlp1[l].astype(_BF16), w_mlp2[l].astype(_BF16),
                  row(ln2_g[l]), row(ln2_b[l])).reshape(b, s, d)
    return x
```

```python
import jax
import jax.numpy as jnp
import numpy as np
from jax import lax
from jax.experimental import pallas as pl
from jax.experimental.pallas import tpu as pltpu

D_MODEL = 1024
DEPTH = 2
POOL_WIDTH = 512
POOL_WINDOWS = (2, 4, 8, 16)
POOL_GROUP_WIDTH = 128
ATTN_WIDTH = 512
HEAD_DIM = 64
N_HEADS = 8
N_PAIRS = N_HEADS // 2
D_FF = 4 * D_MODEL
LN_EPS = 1e-5
NEG_INF = -1e30
ALPHA = float((2.0 * DEPTH) ** 0.25)
LOG2E = 1.4426950408889634

LANES = 128
MXU_DIM = 256
POOL_HALO = 32
GATE_ONE_LANE = 24
GATE_COPIES = 9
GATE_COLS_PER_HEAD = 6
PAIR_WIDTH = 2 * LANES
VT_ROWS = 80
STAT_Q2, STAT_K2, STAT_C_FIRST, STAT_C_LAST, STAT_ROWS = 0, 1, 2, 3, 4
PRUNE_MARGIN = 152.0
PRUNE_NORM_SAFETY = 1.02

T_Q = 2048
T_K = 512
ATT_CW = MXU_DIM
ATT_BLOCKS_PER_ITER = 1
ATT_RING = 4
ATT_LOOKAHEAD = 3
TM_IN = T_K
TM_POST = 1024
POST_SUBTILES = 4
FF_CHUNK = 1024
VMEM_LIMIT = 56 * 1024 * 1024

_BF16 = jnp.bfloat16
_F32 = jnp.float32
_NT = (((1,), (1,)), ((), ()))


def _dot(a, b):
    return jnp.dot(a, b, preferred_element_type=_F32)


def _dot_nt(a, b):
    return lax.dot_general(a, b, _NT, preferred_element_type=_F32)


def _split3(x):
    hi = x.astype(_BF16)
    r = x - hi.astype(_F32)
    mid = r.astype(_BF16)
    lo = (r - mid.astype(_F32)).astype(_BF16)
    return hi, mid, lo


def _inproj_kernel(x_ref, w_ref, wf_ref, bf_ref, pw_ref, ps_ref,
                   eq_ref, ek_ref, hind_ref,
                   pool_ref, qa_ref, ka_ref, vt_ref, st_ref,
                   ubuf, sb1, sb2, sb3, carry, wu_ref, wq_ref, wk_ref, wvt_ref):
    i = pl.program_id(1)
    tm = x_ref.shape[0]
    hl = POOL_HALO

    @pl.when((pl.program_id(0) == 0) & (i == 0))
    def _():
        o_q, o_k, o_v = POOL_WIDTH, POOL_WIDTH + ATTN_WIDTH, POOL_WIDTH + 2 * ATTN_WIDTH
        wu_ref[...] = w_ref[:, 0:o_q].astype(_BF16)
        wq_ref[...] = (w_ref[:, o_q:o_k] * (HEAD_DIM ** -0.5 * LOG2E)).astype(_BF16)
        wk_ref[...] = w_ref[:, o_k:o_v].astype(_BF16)
        wvt_ref[...] = w_ref[:, o_v:o_v + ATTN_WIDTH].T.astype(_BF16)

    @pl.when(i == 0)
    def _():
        ubuf[0:hl, :] = jnp.zeros((hl, POOL_WIDTH), _F32)
        carry[...] = jnp.zeros(carry.shape, _F32)

    @pl.when(i > 0)
    def _():
        ubuf[0:hl, :] = ubuf[tm:tm + hl, :]

    xb = x_ref[...].astype(_BF16)

    lane = lax.broadcasted_iota(jnp.int32, (tm, LANES), 1)
    ng = GATE_ONE_LANE
    f = _dot(xb, wf_ref[...]) + bf_ref[...]
    ubuf[hl:hl + tm, :] = _dot(xb, wu_ref[...])

    lf = (jnp.minimum(f, 0.0) - jnp.log1p(jnp.exp(-jnp.abs(f)))) * LOG2E
    l_hi, l_mid, l_lo = _split3(lf)
    zero = jnp.zeros_like(l_hi)
    pieces = jnp.where(lane < ng, l_hi,
                       jnp.where(lane < 2 * ng, l_mid, jnp.where(lane < 3 * ng, l_lo, zero)))
    row = lax.broadcasted_iota(jnp.int32, (tm, tm), 0)
    col = lax.broadcasted_iota(jnp.int32, (tm, tm), 1)
    tri = jnp.where(col <= row, 1.0, 0.0).astype(_BF16)
    cs = _dot(tri, pieces)
    qf = _dot(xb, wq_ref[...])
    q = qf.astype(_BF16)

    s2 = ubuf[8:tm + 32, :] + ubuf[7:tm + 31, :]
    sb1[8:tm + 32, :] = s2[:, 128:512]
    s4 = sb1[16:tm + 32, :] + sb1[14:tm + 30, :]
    sb2[16:tm + 32, :] = s4[:, 128:384]
    s8 = sb2[24:tm + 32, :] + sb2[20:tm + 28, :]
    sb3[24:tm + 32, :] = s8[:, 128:256]
    s16 = sb3[32:tm + 32, :] + sb3[24:tm + 24, :]
    sums = (s2[24:, 0:128], s4[16:, 0:128], s8[8:, 0:128], s16)

    c = cs + pltpu.roll(cs, LANES - ng, axis=1) + pltpu.roll(cs, LANES - 2 * ng, axis=1)
    c = jnp.where(lane < ng, c, 0.0) + carry[7:8, :]
    carry[...] = c[tm - 8:tm, :]
    c_hi, c_mid, c_lo = _split3(c)
    one = jnp.where(lane == GATE_ONE_LANE, 1.0, 0.0).astype(_BF16)
    gate = jnp.where(lane < 8, c_hi, jnp.where(lane < 16, c_mid, jnp.where(lane < 24, c_lo, one)))
    gq = _dot(gate, eq_ref[...]).astype(_BF16)
    gk = _dot(gate, ek_ref[...]).astype(_BF16)
    kf = _dot(xb, wk_ref[...])
    k = kf.astype(_BF16)

    st_ref[STAT_Q2:STAT_Q2 + 1, :] = jnp.max(
        _dot((qf * qf).astype(_BF16), hind_ref[...]), axis=0, keepdims=True)
    st_ref[STAT_K2:STAT_K2 + 1, :] = jnp.max(
        _dot((kf * kf).astype(_BF16), hind_ref[...]), axis=0, keepdims=True)
    st_ref[STAT_C_FIRST:STAT_C_FIRST + 1, :] = c[0:1, :]
    st_ref[STAT_C_LAST:STAT_C_LAST + 1, :] = c[tm - 1:tm, :]
    st_ref[STAT_ROWS:8, :] = jnp.zeros((8 - STAT_ROWS, LANES), _F32)

    pos = (i * tm + lax.broadcasted_iota(jnp.int32, (tm, 1), 0) + 1).astype(_F32)
    for g, win in enumerate(POOL_WINDOWS):
        sl = slice(g * POOL_GROUP_WIDTH, (g + 1) * POOL_GROUP_WIDTH)
        cnt = jnp.minimum(pos, float(win))
        d = sums[g] / cnt - ubuf[hl:hl + tm, sl]
        y = _dot(d.astype(_BF16), pw_ref[g]) * ps_ref[:, sl]
        pool_ref[:, sl] = y.astype(_BF16)

    vt = _dot_nt(wvt_ref[...], xb)
    pad_rows = VT_ROWS - HEAD_DIM
    ones_row = jnp.where(lax.broadcasted_iota(jnp.int32, (pad_rows, tm), 0) == 0, 1.0, 0.0)
    for h in range(N_HEADS):
        vt_ref[h, 0:HEAD_DIM, :] = vt[h * HEAD_DIM:(h + 1) * HEAD_DIM, :].astype(_BF16)
        vt_ref[h, HEAD_DIM:VT_ROWS, :] = ones_row.astype(_BF16)

    for p in range(N_PAIRS):
        src = slice(p * LANES, (p + 1) * LANES)
        qa_ref[:, p * PAIR_WIDTH:p * PAIR_WIDTH + LANES] = q[:, src]
        qa_ref[:, p * PAIR_WIDTH + LANES:(p + 1) * PAIR_WIDTH] = gq[:, src]
        ka_ref[:, p * PAIR_WIDTH:p * PAIR_WIDTH + LANES] = k[:, src]
        ka_ref[:, p * PAIR_WIDTH + LANES:(p + 1) * PAIR_WIDTH] = gk[:, src]


def _inproj(x, w, wf, bf, pw, ps, eq, ek, hind):
    b, s, d = x.shape
    tm = TM_IN
    const = lambda shape: pl.BlockSpec(shape, lambda bi, i: (0,) * len(shape))
    once = lambda shape: pl.BlockSpec(shape, lambda bi, i: (0,) * len(shape),
                                      pipeline_mode=pl.Buffered(1))
    tile = lambda w: pl.BlockSpec((None, tm, w), lambda bi, i: (bi, i, 0))
    out_shape = [jax.ShapeDtypeStruct((b, s, POOL_WIDTH), _BF16),
                 jax.ShapeDtypeStruct((b, s, N_PAIRS * PAIR_WIDTH), _BF16),
                 jax.ShapeDtypeStruct((b, s, N_PAIRS * PAIR_WIDTH), _BF16),
                 jax.ShapeDtypeStruct((b, N_HEADS, s // tm, VT_ROWS, tm), _BF16),
                 jax.ShapeDtypeStruct((b, s // tm, 8, LANES), _F32)]
    out_specs = [tile(POOL_WIDTH), tile(N_PAIRS * PAIR_WIDTH), tile(N_PAIRS * PAIR_WIDTH),
                 pl.BlockSpec((None, N_HEADS, None, VT_ROWS, tm), lambda bi, i: (bi, 0, i, 0, 0)),
                 pl.BlockSpec((None, None, 8, LANES), lambda bi, i: (bi, i, 0, 0))]
    return pl.pallas_call(
        _inproj_kernel,
        grid=(b, s // tm),
        in_specs=[tile(d), once(w.shape),
                  const(wf.shape), const(bf.shape), const(pw.shape), const(ps.shape),
                  const(eq.shape), const(ek.shape), const(hind.shape)],
        out_specs=out_specs,
        out_shape=out_shape,
        scratch_shapes=[pltpu.VMEM((tm + POOL_HALO, 512), _F32),
                        pltpu.VMEM((tm + POOL_HALO, 384), _F32),
                        pltpu.VMEM((tm + POOL_HALO, 256), _F32),
                        pltpu.VMEM((tm + POOL_HALO, 128), _F32),
                        pltpu.VMEM((8, LANES), _F32),
                        pltpu.VMEM((d, POOL_WIDTH), _BF16),
                        pltpu.VMEM((d, ATTN_WIDTH), _BF16),
                        pltpu.VMEM((d, ATTN_WIDTH), _BF16),
                        pltpu.VMEM((ATTN_WIDTH, d), _BF16)],
        compiler_params=pltpu.CompilerParams(
            dimension_semantics=("arbitrary", "arbitrary"), vmem_limit_bytes=VMEM_LIMIT),
        name="inproj",
    )(x, w, wf, bf, pw, ps, eq, ek, hind)


def _attn_kernel(qa_ref, ka_ref, vt_ref, st_ref, o_ref, qh_sc, m_sc, acc_sc, *rings):
    s_ring, mx_ring = rings[:ATT_RING], rings[ATT_RING:]
    i = pl.program_id(2)
    tq, tk, cw = T_Q, T_K, ATT_CW
    g = GATE_COLS_PER_HEAD
    lane = lax.broadcasted_iota(jnp.int32, (tq, PAIR_WIDTH), 1)
    qf = qa_ref[...].astype(_F32)
    keep0 = (lane < HEAD_DIM) | ((lane >= LANES) & (lane < LANES + g))
    keep1 = ((lane >= HEAD_DIM) & (lane < LANES)) | ((lane >= LANES + g) & (lane < LANES + 2 * g))
    qh_sc[0] = jnp.where(keep0, qf, 0.0).astype(_BF16)
    qh_sc[1] = jnp.where(keep1, qf, 0.0).astype(_BF16)
    m_sc[...] = jnp.full(m_sc.shape, NEG_INF, _F32)
    acc_sc[...] = jnp.zeros(acc_sc.shape, _F32)

    def produce(slot, blk, h, c, mask_off, rows):
        kb = ka_ref[pl.ds(pl.multiple_of(blk * tk, tk), rows), :]
        s = _dot_nt(kb, qh_sc[h, c * cw:(c + 1) * cw, :])
        if mask_off is not None:
            kpos = lax.broadcasted_iota(jnp.int32, s.shape, 0) + mask_off
            qpos = lax.broadcasted_iota(jnp.int32, s.shape, 1) + c * cw
            s = jnp.where(kpos <= qpos, s, NEG_INF)
        s_ring[slot][0:rows, :] = s
        mx_ring[slot][...] = jnp.max(s, axis=0, keepdims=True)

    def consume(slot, blk, h, c, mask_off, rows):
        cs = slice(c * cw, (c + 1) * cw)
        m_prev = m_sc[h, :, cs]
        m_new = jnp.maximum(m_prev, mx_ring[slot][...])
        alpha = jnp.exp2(m_prev - m_new)
        p = jnp.exp2(s_ring[slot][0:rows, :] - m_new).astype(_BF16)
        acc_sc[h, :, cs] = alpha * acc_sc[h, :, cs] + _dot(vt_ref[h, blk, :, 0:rows], p)
        m_sc[h, :, cs] = m_new

    def run(entries, following, pos0=0):
        stream = entries + following[:ATT_LOOKAHEAD]
        for n, unit in enumerate(entries):
            produce((pos0 + n + ATT_LOOKAHEAD) % ATT_RING, *stream[n + ATT_LOOKAHEAD])
            consume((pos0 + n) % ATT_RING, *unit)

    nc = tq // cw
    per_q = tq // tk
    cpb = tk // cw
    units = [(h, c) for c in range(nc) for h in range(2)]
    full_block = lambda blk: [(blk, h, c, None, tk) for h, c in units]
    diag = []
    for r in reversed(range(per_q)):
        for h, c in units:
            if c >= (r + 1) * cpb:
                diag.append((per_q * i + r, h, c, None, tk))
            elif c >= r * cpb:
                diag.append((per_q * i + r, h, c, r * tk, (c - r * cpb + 1) * cw))
    assert len(units) % ATT_RING == 0 and len(diag) % ATT_RING == 0 and ATT_LOOKAHEAD < ATT_RING
    assert per_q % ATT_BLOCKS_PER_ITER == 0

    split = len(diag) - 2 * ATT_LOOKAHEAD
    assert {(h, c) for _, h, c, _, _ in diag[:split]} == set(units)
    for n in range(ATT_LOOKAHEAD):
        produce(n, *diag[n])
    run(diag[:split], diag[split:], 0)

    n_iters = (per_q // ATT_BLOCKS_PER_ITER) * i
    tiles_per_q = tq // TM_IN
    tiles_per_iter = ATT_BLOCKS_PER_ITER * tk // TM_IN
    n_tiles = st_ref.shape[1]
    trow = lax.broadcasted_iota(jnp.int32, (n_tiles, LANES), 0)
    in_q = (trow >= i * tiles_per_q) & (trow < (i + 1) * tiles_per_q)
    q2 = jnp.max(jnp.where(in_q, st_ref[STAT_Q2], 0.0), axis=0, keepdims=True)
    k2 = jnp.max(st_ref[STAT_K2], axis=0, keepdims=True)
    bound = PRUNE_NORM_SAFETY * jnp.sqrt(q2 * k2)
    c_t0 = jnp.max(jnp.where(trow == i * tiles_per_q, st_ref[STAT_C_FIRST], -jnp.inf),
                   axis=0, keepdims=True)
    m_min = jnp.min(m_sc[...])
    prunable = (c_t0 - st_ref[STAT_C_LAST]) + bound <= m_min - PRUNE_MARGIN
    group_end = (trow + 1) % tiles_per_iter == 0
    count = jnp.sum(jnp.where(prunable & group_end, 1.0, 0.0), axis=0, keepdims=True)
    head_lane = lax.broadcasted_iota(jnp.int32, (1, LANES), 1) - 2 * pl.program_id(1)
    n_skip = jnp.minimum(jnp.sum(jnp.where(head_lane == 0, count, 0.0)),
                         jnp.sum(jnp.where(head_lane == 1, count, 0.0))).astype(jnp.int32)
    n_skip = jnp.minimum(n_skip, n_iters)

    run(diag[split:], full_block(ATT_BLOCKS_PER_ITER * n_skip), split)

    def body(j, carry):
        blocks = [ATT_BLOCKS_PER_ITER * j + r for r in range(ATT_BLOCKS_PER_ITER)]
        run(sum((full_block(blk) for blk in blocks), []), full_block(ATT_BLOCKS_PER_ITER * (j + 1)))
        return carry

    lax.fori_loop(n_skip, n_iters, body, 0)

    outs = []
    for h in range(2):
        a = acc_sc[h]
        outs.append(a[0:HEAD_DIM, :] / a[HEAD_DIM:HEAD_DIM + 1, :])
    o_ref[...] = jnp.concatenate(outs, axis=0).T.astype(o_ref.dtype)


def _attention(qa, ka, vt, stats):
    b, s, _ = qa.shape
    tq, tk = T_Q, T_K
    return pl.pallas_call(
        _attn_kernel,
        grid=(b, N_PAIRS, s // tq),
        in_specs=[pl.BlockSpec((None, tq, PAIR_WIDTH), lambda bi, p, i: (bi, i, p)),
                  pl.BlockSpec((None, s, PAIR_WIDTH), lambda bi, p, i: (bi, 0, p)),
                  pl.BlockSpec((None, 2, s // tk, VT_ROWS, tk), lambda bi, p, i: (bi, p, 0, 0, 0)),
                  pl.BlockSpec((None,) + stats.shape[1:], lambda bi, p, i: (bi, 0, 0, 0))],
        out_specs=pl.BlockSpec((None, tq, LANES), lambda bi, p, i: (bi, i, p)),
        out_shape=jax.ShapeDtypeStruct((b, s, ATTN_WIDTH), _BF16),
        scratch_shapes=([pltpu.VMEM((2, tq, PAIR_WIDTH), _BF16),
                         pltpu.VMEM((2, 1, tq), _F32),
                         pltpu.VMEM((2, VT_ROWS, tq), _F32)]
                        + [pltpu.VMEM((tk, ATT_CW), _F32) for _ in range(ATT_RING)]
                        + [pltpu.VMEM((1, ATT_CW), _F32) for _ in range(ATT_RING)]),
        compiler_params=pltpu.CompilerParams(
            dimension_semantics=("arbitrary", "arbitrary", "arbitrary"),
            vmem_limit_bytes=VMEM_LIMIT),
        name="fox_attn",
    )(qa, ka, vt, stats)


def _layer_norm(z, g, b):
    mu = jnp.mean(z, axis=-1, keepdims=True)
    zc = z - mu
    var = jnp.mean(zc * zc, axis=-1, keepdims=True)
    return zc * lax.rsqrt(var + LN_EPS) * g + b


def _post_kernel(x_ref, pool_ref, attn_ref, wop_ref, woa_ref, g1_ref, b1_ref,
                 w1_ref, w2_ref, g2_ref, b2_ref, o_ref):
    sub = x_ref.shape[0] // POST_SUBTILES
    rows = [slice(n * sub, (n + 1) * sub) for n in range(POST_SUBTILES)]

    def out_proj(r):
        return _dot(pool_ref[r, :], wop_ref[...]) + _dot(attn_ref[r, :], woa_ref[...])

    def mlp_chunks(x1b, y, chunks):
        for c in chunks:
            sl = slice(c * FF_CHUNK, (c + 1) * FF_CHUNK)
            hid = jnp.maximum(_dot(x1b, w1_ref[:, sl]), 0.0)
            y = y + _dot((hid * hid).astype(_BF16), w2_ref[sl, :])
        return y

    def ln1(r, mix):
        return _layer_norm(ALPHA * x_ref[r, :] + mix, g1_ref[...], b1_ref[...])

    def ln2_store(r, x1, y):
        o_ref[r, :] = _layer_norm(ALPHA * x1 + y, g2_ref[...], b2_ref[...])

    first, rest = [0], list(range(1, D_FF // FF_CHUNK))
    mix = [out_proj(r) for r in rows]
    x1 = [ln1(rows[0], mix[0])] + [None] * (len(rows) - 1)
    y_prev = None
    for n, r in enumerate(rows):
        x1n_b = x1[n].astype(_BF16)
        y = mlp_chunks(x1n_b, jnp.zeros_like(x1[n]), first)
        if n + 1 < len(rows):
            x1[n + 1] = ln1(rows[n + 1], mix[n + 1])
        if n > 0:
            ln2_store(rows[n - 1], x1[n - 1], y_prev)
        y_prev = mlp_chunks(x1n_b, y, rest)
    ln2_store(rows[-1], x1[-1], y_prev)


def _post(x, pool, attn, wop, woa, g1, b1, w1, w2, g2, b2):
    t, d = x.shape
    tm = TM_POST
    const = lambda a: pl.BlockSpec(a.shape, lambda i: (0,) * a.ndim, pipeline_mode=pl.Buffered(1))
    tile = lambda w: pl.BlockSpec((tm, w), lambda i: (i, 0))
    return pl.pallas_call(
        _post_kernel,
        grid=(t // tm,),
        in_specs=[tile(d), tile(512), tile(512), const(wop), const(woa), const(g1), const(b1),
                  const(w1), const(w2), const(g2), const(b2)],
        out_specs=tile(d),
        out_shape=jax.ShapeDtypeStruct((t, d), _F32),
        compiler_params=pltpu.CompilerParams(
            dimension_semantics=("arbitrary",), vmem_limit_bytes=VMEM_LIMIT),
        name="post",
    )(x, pool, attn, wop, woa, g1, b1, w1, w2, g2, b2)


def _gate_placement():
    eq = np.zeros((LANES, N_PAIRS * LANES), np.float32)
    ek = np.zeros((LANES, N_PAIRS * LANES), np.float32)
    for h in range(N_HEADS):
        base = (h // 2) * LANES + (h % 2) * GATE_COLS_PER_HEAD
        for piece in range(3):
            eq[8 * piece + h, base + piece] = 1.0
            eq[GATE_ONE_LANE, base + 3 + piece] = 1.0
            ek[GATE_ONE_LANE, base + piece] = 1.0
            ek[8 * piece + h, base + 3 + piece] = -1.0
    return jnp.asarray(eq, _BF16), jnp.asarray(ek, _BF16)


def kernel(x, w_in, b_f, pool_w, pool_scale, w_out, ln1_g, ln1_b, w_mlp1, w_mlp2, ln2_g, ln2_b):
    b, s, d = x.shape
    assert s % T_Q == 0 and (b * s) % TM_POST == 0
    eq, ek = _gate_placement()
    hind = jnp.asarray(np.arange(ATTN_WIDTH)[:, None] // HEAD_DIM == np.arange(LANES)[None, :], _BF16)
    o_f = POOL_WIDTH + 3 * ATTN_WIDTH
    row = lambda a: a.reshape(1, -1)
    for l in range(DEPTH):
        w = w_in[l]
        wf = jnp.pad(jnp.tile(w[:, o_f:], (1, GATE_COPIES)),
                     ((0, 0), (0, LANES - GATE_COPIES * N_HEADS))).astype(_BF16)
        bf = jnp.pad(jnp.tile(b_f[l], GATE_COPIES), (0, LANES - GATE_COPIES * N_HEADS)).reshape(1, LANES)
        pool, qa, ka, vt, stats = _inproj(x, w, wf, bf, pool_w[l].astype(_BF16),
                                          row(pool_scale[l]), eq, ek, hind)
        attn = _attention(qa, ka, vt, stats.transpose(0, 2, 1, 3))
        wo = w_out[l].astype(_BF16)
        x = _post(x.reshape(b * s, d), pool.reshape(b * s, -1), attn.reshape(b * s, -1),
                  wo[:POOL_WIDTH], wo[POOL_WIDTH:], row(ln1_g[l]), row(ln1_b[l]),
                  w_mlp1[l].astype(_BF16), w_mlp2[l].astype(_BF16),
                  row(ln2_g[l]), row(ln2_b[l])).reshape(b, s, d)
    return x
```

```python
import jax
import jax.numpy as jnp
import numpy as np
from jax import lax
from jax.experimental import pallas as pl
from jax.experimental.pallas import tpu as pltpu

D_MODEL = 1024
DEPTH = 2
POOL_WIDTH = 512
POOL_WINDOWS = (2, 4, 8, 16)
POOL_GROUP_WIDTH = 128
ATTN_WIDTH = 512
HEAD_DIM = 64
N_HEADS = 8
N_PAIRS = N_HEADS // 2
D_FF = 4 * D_MODEL
LN_EPS = 1e-5
NEG_INF = -1e30
ALPHA = float((2.0 * DEPTH) ** 0.25)
LOG2E = 1.4426950408889634

LANES = 128
MXU_DIM = 256
POOL_HALO = 32
GATE_ONE_LANE = 24
GATE_COPIES = 9
GATE_COLS_PER_HEAD = 6
PAIR_WIDTH = 2 * LANES
VT_ROWS = 80
STAT_Q2, STAT_K2, STAT_C_FIRST, STAT_C_LAST, STAT_ROWS = 0, 1, 2, 3, 4
PRUNE_MARGIN = 152.0
PRUNE_NORM_SAFETY = 1.02

T_Q = 2048
T_K = 512
ATT_CW = MXU_DIM
ATT_BLOCKS_PER_ITER = 1
ATT_RING = 4
ATT_LOOKAHEAD = 3
TM_IN = T_K
TM_POST = 1024
POST_SUBTILES = 4
FF_CHUNK = 1024
VMEM_LIMIT = 56 * 1024 * 1024

_BF16 = jnp.bfloat16
_F32 = jnp.float32
_NT = (((1,), (1,)), ((), ()))


def _dot(a, b):
    return jnp.dot(a, b, preferred_element_type=_F32)


def _dot_nt(a, b):
    return lax.dot_general(a, b, _NT, preferred_element_type=_F32)


def _split3(x):
    hi = x.astype(_BF16)
    r = x - hi.astype(_F32)
    mid = r.astype(_BF16)
    lo = (r - mid.astype(_F32)).astype(_BF16)
    return hi, mid, lo


def _inproj_kernel(x_ref, wu_ref, wq_ref, wk_ref, wvt_ref, wf_ref, bf_ref, pw_ref, ps_ref,
                   eg_ref, hind_ref,
                   pool_ref, qa_ref, ka_ref, vt_ref, st_ref,
                   ubuf, sb1, sb2, sb3, carry):
    i = pl.program_id(1)
    tm = x_ref.shape[0]
    hl = POOL_HALO

    @pl.when(i == 0)
    def _():
        ubuf[0:hl, :] = jnp.zeros((hl, POOL_WIDTH), _F32)
        carry[...] = jnp.zeros(carry.shape, _F32)

    @pl.when(i > 0)
    def _():
        ubuf[0:hl, :] = ubuf[tm:tm + hl, :]

    xb = x_ref[...].astype(_BF16)

    lane = lax.broadcasted_iota(jnp.int32, (tm, LANES), 1)
    ng = GATE_ONE_LANE
    f = _dot(xb, wf_ref[...]) + bf_ref[...]
    ubuf[hl:hl + tm, :] = _dot(xb, wu_ref[...])

    lf = (jnp.minimum(f, 0.0) - jnp.log1p(jnp.exp(-jnp.abs(f)))) * LOG2E
    l_hi, l_mid, l_lo = _split3(lf)
    zero = jnp.zeros_like(l_hi)
    pieces = jnp.where(lane < ng, l_hi,
                       jnp.where(lane < 2 * ng, l_mid, jnp.where(lane < 3 * ng, l_lo, zero)))
    row = lax.broadcasted_iota(jnp.int32, (tm, tm), 0)
    col = lax.broadcasted_iota(jnp.int32, (tm, tm), 1)
    tri = jnp.where(col <= row, 1.0, 0.0).astype(_BF16)
    cs = _dot(tri, pieces)
    qf = _dot(xb, wq_ref[...])
    q = qf.astype(_BF16)

    s2 = ubuf[8:tm + 32, :] + ubuf[7:tm + 31, :]
    sb1[8:tm + 32, :] = s2[:, 128:512]
    s4 = sb1[16:tm + 32, :] + sb1[14:tm + 30, :]
    sb2[16:tm + 32, :] = s4[:, 128:384]
    s8 = sb2[24:tm + 32, :] + sb2[20:tm + 28, :]
    sb3[24:tm + 32, :] = s8[:, 128:256]
    s16 = sb3[32:tm + 32, :] + sb3[24:tm + 24, :]
    sums = (s2[24:, 0:128], s4[16:, 0:128], s8[8:, 0:128], s16)

    c = cs + pltpu.roll(cs, LANES - ng, axis=1) + pltpu.roll(cs, LANES - 2 * ng, axis=1)
    c = jnp.where(lane < ng, c, 0.0) + carry[7:8, :]
    carry[...] = c[tm - 8:tm, :]
    c_hi, c_mid, c_lo = _split3(c)
    one = jnp.where(lane == GATE_ONE_LANE, 1.0, 0.0).astype(_BF16)
    gate = jnp.where(lane < 8, c_hi, jnp.where(lane < 16, c_mid, jnp.where(lane < 24, c_lo, one)))
    gqk = _dot(gate, eg_ref[...]).astype(_BF16)
    gq, gk = gqk[:, 0:LANES], gqk[:, LANES:2 * LANES]
    kf = _dot(xb, wk_ref[...])
    k = kf.astype(_BF16)

    st_ref[STAT_Q2:STAT_Q2 + 1, :] = jnp.max(
        _dot((qf * qf).astype(_BF16), hind_ref[...]), axis=0, keepdims=True)
    st_ref[STAT_K2:STAT_K2 + 1, :] = jnp.max(
        _dot((kf * kf).astype(_BF16), hind_ref[...]), axis=0, keepdims=True)
    st_ref[STAT_C_FIRST:STAT_C_FIRST + 1, :] = c[0:1, :]
    st_ref[STAT_C_LAST:STAT_C_LAST + 1, :] = c[tm - 1:tm, :]
    st_ref[STAT_ROWS:8, :] = jnp.zeros((8 - STAT_ROWS, LANES), _F32)

    pos = (i * tm + lax.broadcasted_iota(jnp.int32, (tm, 1), 0) + 1).astype(_F32)
    for g, win in enumerate(POOL_WINDOWS):
        sl = slice(g * POOL_GROUP_WIDTH, (g + 1) * POOL_GROUP_WIDTH)
        cnt = jnp.minimum(pos, float(win))
        d = sums[g] / cnt - ubuf[hl:hl + tm, sl]
        y = _dot(d.astype(_BF16), pw_ref[g]) * ps_ref[:, sl]
        pool_ref[:, sl] = y.astype(_BF16)

    vt = _dot_nt(wvt_ref[...], xb)
    pad_rows = VT_ROWS - HEAD_DIM
    ones_row = jnp.where(lax.broadcasted_iota(jnp.int32, (pad_rows, tm), 0) == 0, 1.0, 0.0)
    for h in range(N_HEADS):
        vt_ref[h, 0:HEAD_DIM, :] = vt[h * HEAD_DIM:(h + 1) * HEAD_DIM, :].astype(_BF16)
        vt_ref[h, HEAD_DIM:VT_ROWS, :] = ones_row.astype(_BF16)

    for p in range(N_PAIRS):
        src = slice(p * LANES, (p + 1) * LANES)
        qa_ref[:, p * PAIR_WIDTH:p * PAIR_WIDTH + LANES] = q[:, src]
        qa_ref[:, p * PAIR_WIDTH + LANES:(p + 1) * PAIR_WIDTH] = gq
        ka_ref[:, p * PAIR_WIDTH:p * PAIR_WIDTH + LANES] = k[:, src]
        ka_ref[:, p * PAIR_WIDTH + LANES:(p + 1) * PAIR_WIDTH] = gk


def _inproj(x, wu, wq, wk, wvt, wf, bf, pw, ps, eg, hind):
    b, s, d = x.shape
    tm = TM_IN
    const = lambda shape: pl.BlockSpec(shape, lambda bi, i: (0,) * len(shape))
    tile = lambda w: pl.BlockSpec((None, tm, w), lambda bi, i: (bi, i, 0))
    out_shape = [jax.ShapeDtypeStruct((b, s, POOL_WIDTH), _BF16),
                 jax.ShapeDtypeStruct((b, s, N_PAIRS * PAIR_WIDTH), _BF16),
                 jax.ShapeDtypeStruct((b, s, N_PAIRS * PAIR_WIDTH), _BF16),
                 jax.ShapeDtypeStruct((b, N_HEADS, s // tm, VT_ROWS, tm), _BF16),
                 jax.ShapeDtypeStruct((b, s // tm, 8, LANES), _F32)]
    out_specs = [tile(POOL_WIDTH), tile(N_PAIRS * PAIR_WIDTH), tile(N_PAIRS * PAIR_WIDTH),
                 pl.BlockSpec((None, N_HEADS, None, VT_ROWS, tm), lambda bi, i: (bi, 0, i, 0, 0)),
                 pl.BlockSpec((None, None, 8, LANES), lambda bi, i: (bi, i, 0, 0))]
    return pl.pallas_call(
        _inproj_kernel,
        grid=(b, s // tm),
        in_specs=[tile(d), const(wu.shape), const(wq.shape), const(wk.shape), const(wvt.shape),
                  const(wf.shape), const(bf.shape), const(pw.shape), const(ps.shape),
                  const(eg.shape), const(hind.shape)],
        out_specs=out_specs,
        out_shape=out_shape,
        scratch_shapes=[pltpu.VMEM((tm + POOL_HALO, 512), _F32),
                        pltpu.VMEM((tm + POOL_HALO, 384), _F32),
                        pltpu.VMEM((tm + POOL_HALO, 256), _F32),
                        pltpu.VMEM((tm + POOL_HALO, 128), _F32),
                        pltpu.VMEM((8, LANES), _F32)],
        compiler_params=pltpu.CompilerParams(
            dimension_semantics=("arbitrary", "arbitrary"), vmem_limit_bytes=VMEM_LIMIT),
        name="inproj",
    )(x, wu, wq, wk, wvt, wf, bf, pw, ps, eg, hind)


def _attn_kernel(qa_ref, ka_ref, vt_ref, st_ref, o_ref, qh_sc, m_sc, acc_sc, *rings):
    s_ring, mx_ring = rings[:ATT_RING], rings[ATT_RING:]
    i = pl.program_id(2)
    tq, tk, cw = T_Q, T_K, ATT_CW
    g = GATE_COLS_PER_HEAD
    lane = lax.broadcasted_iota(jnp.int32, (tq, PAIR_WIDTH), 1).astype(_F32).astype(_BF16)
    qv = qa_ref[...]
    g0 = LANES + 2 * g * pl.program_id(1)
    edge = [jnp.full((1, PAIR_WIDTH), g0 + n * g, jnp.int32).astype(_F32).astype(_BF16)
            for n in range(3)]
    keep0 = (lane < HEAD_DIM) | ((lane >= edge[0]) & (lane < edge[1]))
    keep1 = ((lane >= HEAD_DIM) & (lane < LANES)) | ((lane >= edge[1]) & (lane < edge[2]))
    qh_sc[0] = jnp.where(keep0, qv, jnp.zeros_like(qv))
    qh_sc[1] = jnp.where(keep1, qv, jnp.zeros_like(qv))
    m_sc[...] = jnp.full(m_sc.shape, NEG_INF, _F32)
    acc_sc[...] = jnp.zeros(acc_sc.shape, _F32)

    def produce(slot, blk, h, c, mask_off, rows):
        kb = ka_ref[pl.ds(pl.multiple_of(blk * tk, tk), rows), :]
        s = _dot_nt(kb, qh_sc[h, c * cw:(c + 1) * cw, :])
        if mask_off is not None:
            kpos = lax.broadcasted_iota(jnp.int32, s.shape, 0) + mask_off
            qpos = lax.broadcasted_iota(jnp.int32, s.shape, 1) + c * cw
            s = jnp.where(kpos <= qpos, s, NEG_INF)
        s_ring[slot][0:rows, :] = s
        mx_ring[slot][...] = jnp.max(s, axis=0, keepdims=True)

    def consume(slot, blk, h, c, mask_off, rows):
        cs = slice(c * cw, (c + 1) * cw)
        m_prev = m_sc[h, :, cs]
        m_new = jnp.maximum(m_prev, mx_ring[slot][...])
        alpha = jnp.exp2(m_prev - m_new)
        p = jnp.exp2(s_ring[slot][0:rows, :] - m_new).astype(_BF16)
        acc_sc[h, :, cs] = alpha * acc_sc[h, :, cs] + _dot(vt_ref[h, blk, :, 0:rows], p)
        m_sc[h, :, cs] = m_new

    def run(entries, following, pos0=0):
        stream = entries + following[:ATT_LOOKAHEAD]
        for n, unit in enumerate(entries):
            produce((pos0 + n + ATT_LOOKAHEAD) % ATT_RING, *stream[n + ATT_LOOKAHEAD])
            consume((pos0 + n) % ATT_RING, *unit)

    nc = tq // cw
    per_q = tq // tk
    cpb = tk // cw
    units = [(h, c) for c in range(nc) for h in range(2)]
    diag = []
    for r in reversed(range(per_q)):
        for h, c in units:
            if c >= (r + 1) * cpb:
                diag.append((per_q * i + r, h, c, None, tk))
            elif c >= r * cpb:
                diag.append((per_q * i + r, h, c, r * tk, (c - r * cpb + 1) * cw))
    assert nc % ATT_RING == 0 and len(diag) % ATT_RING == 0 and ATT_LOOKAHEAD < ATT_RING <= nc
    assert ATT_BLOCKS_PER_ITER == 1

    split = len(diag) - 2 * ATT_LOOKAHEAD
    assert {(h, c) for _, h, c, _, _ in diag[:split]} == set(units)
    for n in range(ATT_LOOKAHEAD):
        produce(n, *diag[n])
    run(diag[:split], diag[split:], 0)

    n_iters = (per_q // ATT_BLOCKS_PER_ITER) * i
    tiles_per_q = tq // TM_IN
    tiles_per_iter = ATT_BLOCKS_PER_ITER * tk // TM_IN
    n_tiles = st_ref.shape[1]
    trow = lax.broadcasted_iota(jnp.int32, (n_tiles, LANES), 0)
    in_q = (trow >= i * tiles_per_q) & (trow < (i + 1) * tiles_per_q)
    q2 = jnp.max(jnp.where(in_q, st_ref[STAT_Q2], 0.0), axis=0, keepdims=True)
    k2 = jnp.max(st_ref[STAT_K2], axis=0, keepdims=True)
    bound = PRUNE_NORM_SAFETY * jnp.sqrt(q2 * k2)
    c_t0 = jnp.max(jnp.where(trow == i * tiles_per_q, st_ref[STAT_C_FIRST], -jnp.inf),
                   axis=0, keepdims=True)
    head_lane = lax.broadcasted_iota(jnp.int32, (1, LANES), 1) - 2 * pl.program_id(1)
    m_min = jnp.where(head_lane == 0, jnp.min(m_sc[0]),
                      jnp.where(head_lane == 1, jnp.min(m_sc[1]), -jnp.inf))
    prunable = (c_t0 - st_ref[STAT_C_LAST]) + bound <= m_min - PRUNE_MARGIN
    group_end = (trow + 1) % tiles_per_iter == 0
    count = jnp.sum(jnp.where(prunable & group_end, 1.0, 0.0), axis=0, keepdims=True)
    skip = [jnp.minimum(jnp.sum(jnp.where(head_lane == h, count, 0.0)).astype(jnp.int32), n_iters)
            for h in range(2)]
    lo, hi = jnp.minimum(skip[0], skip[1]), jnp.maximum(skip[0], skip[1])
    far = jnp.where(skip[0] <= skip[1], 0, 1)
    one_head = lambda blk, h: [(blk, h, c, None, tk) for c in range(nc)]

    run(diag[split:], one_head(lo, jnp.where(lo < hi, far, 0)), split)

    def body_one(j, carry):
        run(one_head(j, far), one_head(j + 1, jnp.where(j + 1 < hi, far, 0)))
        return carry

    lax.fori_loop(lo, hi, body_one, 0)

    def body_both(j, carry):
        run(one_head(j, 0) + one_head(j, 1), one_head(j + 1, 0))
        return carry

    lax.fori_loop(hi, n_iters, body_both, 0)

    outs = []
    for h in range(2):
        a = acc_sc[h]
        outs.append(a[0:HEAD_DIM, :] / a[HEAD_DIM:HEAD_DIM + 1, :])
    o_ref[...] = jnp.concatenate(outs, axis=0).T.astype(o_ref.dtype)


def _attention(qa, ka, vt, stats):
    b, s, _ = qa.shape
    tq, tk = T_Q, T_K
    return pl.pallas_call(
        _attn_kernel,
        grid=(b, N_PAIRS, s // tq),
        in_specs=[pl.BlockSpec((None, tq, PAIR_WIDTH), lambda bi, p, i: (bi, i, p)),
                  pl.BlockSpec((None, s, PAIR_WIDTH), lambda bi, p, i: (bi, 0, p)),
                  pl.BlockSpec((None, 2, s // tk, VT_ROWS, tk), lambda bi, p, i: (bi, p, 0, 0, 0)),
                  pl.BlockSpec((None,) + stats.shape[1:], lambda bi, p, i: (bi, 0, 0, 0))],
        out_specs=pl.BlockSpec((None, tq, LANES), lambda bi, p, i: (bi, i, p)),
        out_shape=jax.ShapeDtypeStruct((b, s, ATTN_WIDTH), _BF16),
        scratch_shapes=([pltpu.VMEM((2, tq, PAIR_WIDTH), _BF16),
                         pltpu.VMEM((2, 1, tq), _F32),
                         pltpu.VMEM((2, VT_ROWS, tq), _F32)]
                        + [pltpu.VMEM((tk, ATT_CW), _F32) for _ in range(ATT_RING)]
                        + [pltpu.VMEM((1, ATT_CW), _F32) for _ in range(ATT_RING)]),
        compiler_params=pltpu.CompilerParams(
            dimension_semantics=("arbitrary", "arbitrary", "arbitrary"),
            vmem_limit_bytes=VMEM_LIMIT),
        name="fox_attn",
    )(qa, ka, vt, stats)


def _layer_norm(z, g, b):
    mu = jnp.mean(z, axis=-1, keepdims=True)
    zc = z - mu
    var = jnp.mean(zc * zc, axis=-1, keepdims=True)
    return zc * lax.rsqrt(var + LN_EPS) * g + b


def _post_kernel(x_ref, pool_ref, attn_ref, wop_ref, woa_ref, g1_ref, b1_ref,
                 w1_ref, w2_ref, g2_ref, b2_ref, o_ref):
    sub = x_ref.shape[0] // POST_SUBTILES
    rows = [slice(n * sub, (n + 1) * sub) for n in range(POST_SUBTILES)]

    def out_proj(r):
        return _dot(pool_ref[r, :], wop_ref[...]) + _dot(attn_ref[r, :], woa_ref[...])

    def mlp_chunks(x1b, y, chunks):
        for c in chunks:
            sl = slice(c * FF_CHUNK, (c + 1) * FF_CHUNK)
            hid = jnp.maximum(_dot(x1b, w1_ref[:, sl]), 0.0)
            y = y + _dot((hid * hid).astype(_BF16), w2_ref[sl, :])
        return y

    def ln1(r, mix):
        return _layer_norm(ALPHA * x_ref[r, :] + mix, g1_ref[...], b1_ref[...])

    def ln2_store(r, x1, y):
        o_ref[r, :] = _layer_norm(ALPHA * x1 + y, g2_ref[...], b2_ref[...])

    first, rest = [0], list(range(1, D_FF // FF_CHUNK))
    mix = [out_proj(r) for r in rows]
    x1 = [ln1(rows[0], mix[0])] + [None] * (len(rows) - 1)
    y_prev = None
    for n, r in enumerate(rows):
        x1n_b = x1[n].astype(_BF16)
        y = mlp_chunks(x1n_b, jnp.zeros_like(x1[n]), first)
        if n + 1 < len(rows):
            x1[n + 1] = ln1(rows[n + 1], mix[n + 1])
        if n > 0:
            ln2_store(rows[n - 1], x1[n - 1], y_prev)
        y_prev = mlp_chunks(x1n_b, y, rest)
    ln2_store(rows[-1], x1[-1], y_prev)


def _post(x, pool, attn, wop, woa, g1, b1, w1, w2, g2, b2):
    t, d = x.shape
    tm = TM_POST
    const = lambda a: pl.BlockSpec(a.shape, lambda i: (0,) * a.ndim, pipeline_mode=pl.Buffered(1))
    tile = lambda w: pl.BlockSpec((tm, w), lambda i: (i, 0))
    return pl.pallas_call(
        _post_kernel,
        grid=(t // tm,),
        in_specs=[tile(d), tile(512), tile(512), const(wop), const(woa), const(g1), const(b1),
                  const(w1), const(w2), const(g2), const(b2)],
        out_specs=tile(d),
        out_shape=jax.ShapeDtypeStruct((t, d), _F32),
        compiler_params=pltpu.CompilerParams(
            dimension_semantics=("arbitrary",), vmem_limit_bytes=VMEM_LIMIT),
        name="post",
    )(x, pool, attn, wop, woa, g1, b1, w1, w2, g2, b2)


def _gate_placement():
    eq = np.zeros((LANES, LANES), np.float32)
    ek = np.zeros((LANES, LANES), np.float32)
    for h in range(N_HEADS):
        base = h * GATE_COLS_PER_HEAD
        for piece in range(3):
            eq[8 * piece + h, base + piece] = 1.0
            eq[GATE_ONE_LANE, base + 3 + piece] = 1.0
            ek[GATE_ONE_LANE, base + piece] = 1.0
            ek[8 * piece + h, base + 3 + piece] = -1.0
    return jnp.asarray(np.concatenate([eq, ek], axis=1), _BF16)


def kernel(x, w_in, b_f, pool_w, pool_scale, w_out, ln1_g, ln1_b, w_mlp1, w_mlp2, ln2_g, ln2_b):
    b, s, d = x.shape
    assert s % T_Q == 0 and (b * s) % TM_POST == 0
    eg = _gate_placement()
    hind = jnp.asarray(np.arange(ATTN_WIDTH)[:, None] // HEAD_DIM == np.arange(LANES)[None, :], _BF16)
    o_q = POOL_WIDTH
    o_k = o_q + ATTN_WIDTH
    o_v = o_k + ATTN_WIDTH
    o_f = o_v + ATTN_WIDTH
    row = lambda a: a.reshape(1, -1)
    for l in range(DEPTH):
        w = w_in[l]
        wu = w[:, :o_q].astype(_BF16)
        wq = (w[:, o_q:o_k] * (HEAD_DIM ** -0.5 * LOG2E)).astype(_BF16)
        wk = w[:, o_k:o_v].astype(_BF16)
        wvt = w[:, o_v:o_f].T.astype(_BF16)
        wf = jnp.pad(jnp.tile(w[:, o_f:], (1, GATE_COPIES)),
                     ((0, 0), (0, LANES - GATE_COPIES * N_HEADS))).astype(_BF16)
        bf = jnp.pad(jnp.tile(b_f[l], GATE_COPIES), (0, LANES - GATE_COPIES * N_HEADS)).reshape(1, LANES)
        pool, qa, ka, vt, stats = _inproj(x, wu, wq, wk, wvt, wf, bf, pool_w[l].astype(_BF16),
                                          row(pool_scale[l]), eg, hind)
        attn = _attention(qa, ka, vt, stats.transpose(0, 2, 1, 3))
        wo = w_out[l].astype(_BF16)
        x = _post(x.reshape(b * s, d), pool.reshape(b * s, -1), attn.reshape(b * s, -1),
                  wo[:POOL_WIDTH], wo[POOL_WIDTH:], row(ln1_g[l]), row(ln1_b[l]),
                  w_mlp1[l].astype(_BF16), w_mlp2[l].astype(_BF16),
                  row(ln2_g[l]), row(ln2_b[l])).reshape(b, s, d)
    return x
```

```python
import jax
import jax.numpy as jnp
import numpy as np
from jax import lax
from jax.experimental import pallas as pl
from jax.experimental.pallas import tpu as pltpu

D_MODEL = 1024
DEPTH = 2
POOL_WIDTH = 512
POOL_WINDOWS = (2, 4, 8, 16)
POOL_GROUP_WIDTH = 128
ATTN_WIDTH = 512
HEAD_DIM = 64
N_HEADS = 8
N_PAIRS = N_HEADS // 2
D_FF = 4 * D_MODEL
LN_EPS = 1e-5
NEG_INF = -1e30
ALPHA = float((2.0 * DEPTH) ** 0.25)
LOG2E = 1.4426950408889634

LANES = 128
MXU_DIM = 256
POOL_HALO = 32
GATE_ONE_LANE = 24
GATE_COPIES = 9
GATE_COLS_PER_HEAD = 6
PAIR_WIDTH = 2 * LANES
VT_ROWS = 80
STAT_Q2, STAT_K2, STAT_C_FIRST, STAT_C_LAST, STAT_ROWS = 0, 1, 2, 3, 4
PRUNE_MARGIN = 152.0
PRUNE_NORM_SAFETY = 1.02

T_Q = 2048
T_K = 512
ATT_CW = MXU_DIM
ATT_BLOCKS_PER_ITER = 1
ATT_RING = 4
ATT_LOOKAHEAD = 3
TM_IN = T_K
TM_POST = 1024
POST_SUBTILES = 4
FF_CHUNK = 1024
_MIB = 1024 * 1024
VMEM_LIMIT_INPROJ = 40 * _MIB
VMEM_LIMIT_ATTN = 32 * _MIB
VMEM_LIMIT_POST = 52 * _MIB

_BF16 = jnp.bfloat16
_F32 = jnp.float32
_NT = (((1,), (1,)), ((), ()))


def _dot(a, b):
    return jnp.dot(a, b, preferred_element_type=_F32)


def _dot_nt(a, b):
    return lax.dot_general(a, b, _NT, preferred_element_type=_F32)


def _split3(x):
    hi = x.astype(_BF16)
    r = x - hi.astype(_F32)
    mid = r.astype(_BF16)
    lo = (r - mid.astype(_F32)).astype(_BF16)
    return hi, mid, lo


def _inproj_kernel(x_ref, wu_ref, wq_ref, wk_ref, wvt_ref, wf_ref, bf_ref, pw_ref, ps_ref,
                   eg_ref, hind_ref,
                   pool_ref, qa_ref, ka_ref, vt_ref, st_ref,
                   ubuf, sb1, sb2, sb3, carry):
    i = pl.program_id(1)
    tm = x_ref.shape[0]
    hl = POOL_HALO

    @pl.when(i == 0)
    def _():
        ubuf[0:hl, :] = jnp.zeros((hl, POOL_WIDTH), _F32)
        carry[...] = jnp.zeros(carry.shape, _F32)

    @pl.when(i > 0)
    def _():
        ubuf[0:hl, :] = ubuf[tm:tm + hl, :]

    xb = x_ref[...].astype(_BF16)

    lane = lax.broadcasted_iota(jnp.int32, (tm, LANES), 1)
    ng = GATE_ONE_LANE
    f = _dot(xb, wf_ref[...]) + bf_ref[...]
    ubuf[hl:hl + tm, :] = _dot(xb, wu_ref[...])

    lf = (jnp.minimum(f, 0.0) - jnp.log1p(jnp.exp(-jnp.abs(f)))) * LOG2E
    l_hi, l_mid, l_lo = _split3(lf)
    zero = jnp.zeros_like(l_hi)
    pieces = jnp.where(lane < ng, l_hi,
                       jnp.where(lane < 2 * ng, l_mid, jnp.where(lane < 3 * ng, l_lo, zero)))
    row = lax.broadcasted_iota(jnp.int32, (tm, tm), 0)
    col = lax.broadcasted_iota(jnp.int32, (tm, tm), 1)
    tri = jnp.where(col <= row, 1.0, 0.0).astype(_BF16)
    cs = _dot(tri, pieces)
    qf = _dot(xb, wq_ref[...])
    q = qf.astype(_BF16)

    s2 = ubuf[8:tm + 32, :] + ubuf[7:tm + 31, :]
    sb1[8:tm + 32, :] = s2[:, 128:512]
    s4 = sb1[16:tm + 32, :] + sb1[14:tm + 30, :]
    sb2[16:tm + 32, :] = s4[:, 128:384]
    s8 = sb2[24:tm + 32, :] + sb2[20:tm + 28, :]
    sb3[24:tm + 32, :] = s8[:, 128:256]
    s16 = sb3[32:tm + 32, :] + sb3[24:tm + 24, :]
    sums = (s2[24:, 0:128], s4[16:, 0:128], s8[8:, 0:128], s16)

    c = cs + pltpu.roll(cs, LANES - ng, axis=1) + pltpu.roll(cs, LANES - 2 * ng, axis=1)
    c = jnp.where(lane < ng, c, 0.0) + carry[7:8, :]
    carry[...] = c[tm - 8:tm, :]
    c_hi, c_mid, c_lo = _split3(c)
    one = jnp.where(lane == GATE_ONE_LANE, 1.0, 0.0).astype(_BF16)
    gate = jnp.where(lane < 8, c_hi, jnp.where(lane < 16, c_mid, jnp.where(lane < 24, c_lo, one)))
    gqk = _dot(gate, eg_ref[...]).astype(_BF16)
    gq, gk = gqk[:, 0:LANES], gqk[:, LANES:2 * LANES]
    kf = _dot(xb, wk_ref[...])
    k = kf.astype(_BF16)

    st_ref[STAT_Q2:STAT_Q2 + 1, :] = jnp.max(
        _dot((qf * qf).astype(_BF16), hind_ref[...]), axis=0, keepdims=True)
    st_ref[STAT_K2:STAT_K2 + 1, :] = jnp.max(
        _dot((kf * kf).astype(_BF16), hind_ref[...]), axis=0, keepdims=True)
    st_ref[STAT_C_FIRST:STAT_C_FIRST + 1, :] = c[0:1, :]
    st_ref[STAT_C_LAST:STAT_C_LAST + 1, :] = c[tm - 1:tm, :]
    st_ref[STAT_ROWS:8, :] = jnp.zeros((8 - STAT_ROWS, LANES), _F32)

    pos = (i * tm + lax.broadcasted_iota(jnp.int32, (tm, 1), 0) + 1).astype(_F32)
    for g, win in enumerate(POOL_WINDOWS):
        sl = slice(g * POOL_GROUP_WIDTH, (g + 1) * POOL_GROUP_WIDTH)
        cnt = jnp.minimum(pos, float(win))
        d = sums[g] / cnt - ubuf[hl:hl + tm, sl]
        y = _dot(d.astype(_BF16), pw_ref[g]) * ps_ref[:, sl]
        pool_ref[:, sl] = y.astype(_BF16)

    vt = _dot_nt(wvt_ref[...], xb)
    pad_rows = VT_ROWS - HEAD_DIM
    ones_row = jnp.where(lax.broadcasted_iota(jnp.int32, (pad_rows, tm), 0) == 0, 1.0, 0.0)
    for h in range(N_HEADS):
        vt_ref[h, 0:HEAD_DIM, :] = vt[h * HEAD_DIM:(h + 1) * HEAD_DIM, :].astype(_BF16)
        vt_ref[h, HEAD_DIM:VT_ROWS, :] = ones_row.astype(_BF16)

    for p in range(N_PAIRS):
        src = slice(p * LANES, (p + 1) * LANES)
        qa_ref[:, p * PAIR_WIDTH:p * PAIR_WIDTH + LANES] = q[:, src]
        qa_ref[:, p * PAIR_WIDTH + LANES:(p + 1) * PAIR_WIDTH] = gq
        ka_ref[:, p * PAIR_WIDTH:p * PAIR_WIDTH + LANES] = k[:, src]
        ka_ref[:, p * PAIR_WIDTH + LANES:(p + 1) * PAIR_WIDTH] = gk


def _inproj(x, wu, wq, wk, wvt, wf, bf, pw, ps, eg, hind):
    b, s, d = x.shape
    tm = TM_IN
    const = lambda shape: pl.BlockSpec(shape, lambda bi, i: (0,) * len(shape))
    tile = lambda w: pl.BlockSpec((None, tm, w), lambda bi, i: (bi, i, 0))
    out_shape = [jax.ShapeDtypeStruct((b, s, POOL_WIDTH), _BF16),
                 jax.ShapeDtypeStruct((b, s, N_PAIRS * PAIR_WIDTH), _BF16),
                 jax.ShapeDtypeStruct((b, s, N_PAIRS * PAIR_WIDTH), _BF16),
                 jax.ShapeDtypeStruct((b, N_HEADS, s // tm, VT_ROWS, tm), _BF16),
                 jax.ShapeDtypeStruct((b, s // tm, 8, LANES), _F32)]
    out_specs = [tile(POOL_WIDTH), tile(N_PAIRS * PAIR_WIDTH), tile(N_PAIRS * PAIR_WIDTH),
                 pl.BlockSpec((None, N_HEADS, None, VT_ROWS, tm), lambda bi, i: (bi, 0, i, 0, 0)),
                 pl.BlockSpec((None, None, 8, LANES), lambda bi, i: (bi, i, 0, 0))]
    return pl.pallas_call(
        _inproj_kernel,
        grid=(b, s // tm),
        in_specs=[tile(d), const(wu.shape), const(wq.shape), const(wk.shape), const(wvt.shape),
                  const(wf.shape), const(bf.shape), const(pw.shape), const(ps.shape),
                  const(eg.shape), const(hind.shape)],
        out_specs=out_specs,
        out_shape=out_shape,
        scratch_shapes=[pltpu.VMEM((tm + POOL_HALO, 512), _F32),
                        pltpu.VMEM((tm + POOL_HALO, 384), _F32),
                        pltpu.VMEM((tm + POOL_HALO, 256), _F32),
                        pltpu.VMEM((tm + POOL_HALO, 128), _F32),
                        pltpu.VMEM((8, LANES), _F32)],
        compiler_params=pltpu.CompilerParams(
            dimension_semantics=("arbitrary", "arbitrary"), vmem_limit_bytes=VMEM_LIMIT_INPROJ),
        name="inproj",
    )(x, wu, wq, wk, wvt, wf, bf, pw, ps, eg, hind)


def _attn_kernel(qa_ref, ka_ref, vt_ref, st_ref, o_ref, qh_sc, m_sc, acc_sc, *rings):
    s_ring, mx_ring = rings[:ATT_RING], rings[ATT_RING:]
    i = pl.program_id(2)
    tq, tk, cw = T_Q, T_K, ATT_CW
    g = GATE_COLS_PER_HEAD
    lane = lax.broadcasted_iota(jnp.int32, (tq, PAIR_WIDTH), 1).astype(_F32).astype(_BF16)
    qv = qa_ref[...]
    g0 = LANES + 2 * g * pl.program_id(1)
    edge = [jnp.full((1, PAIR_WIDTH), g0 + n * g, jnp.int32).astype(_F32).astype(_BF16)
            for n in range(3)]
    keep0 = (lane < HEAD_DIM) | ((lane >= edge[0]) & (lane < edge[1]))
    keep1 = ((lane >= HEAD_DIM) & (lane < LANES)) | ((lane >= edge[1]) & (lane < edge[2]))
    qh_sc[0] = jnp.where(keep0, qv, jnp.zeros_like(qv))
    qh_sc[1] = jnp.where(keep1, qv, jnp.zeros_like(qv))
    m_sc[...] = jnp.full(m_sc.shape, NEG_INF, _F32)
    acc_sc[...] = jnp.zeros(acc_sc.shape, _F32)

    def produce(slot, blk, h, c, mask_off, rows):
        kb = ka_ref[pl.ds(pl.multiple_of(blk * tk, tk), rows), :]
        s = _dot_nt(kb, qh_sc[h, c * cw:(c + 1) * cw, :])
        if mask_off is not None:
            kpos = lax.broadcasted_iota(jnp.int32, s.shape, 0) + mask_off
            qpos = lax.broadcasted_iota(jnp.int32, s.shape, 1) + c * cw
            s = jnp.where(kpos <= qpos, s, NEG_INF)
        s_ring[slot][0:rows, :] = s
        mx_ring[slot][...] = jnp.max(s, axis=0, keepdims=True)

    def consume(slot, blk, h, c, mask_off, rows):
        cs = slice(c * cw, (c + 1) * cw)
        m_prev = m_sc[h, :, cs]
        m_new = jnp.maximum(m_prev, mx_ring[slot][...])
        alpha = jnp.exp2(m_prev - m_new)
        p = jnp.exp2(s_ring[slot][0:rows, :] - m_new).astype(_BF16)
        acc_sc[h, :, cs] = alpha * acc_sc[h, :, cs] + _dot(vt_ref[h, blk, :, 0:rows], p)
        m_sc[h, :, cs] = m_new

    def run(entries, following, pos0=0):
        stream = entries + following[:ATT_LOOKAHEAD]
        for n, unit in enumerate(entries):
            produce((pos0 + n + ATT_LOOKAHEAD) % ATT_RING, *stream[n + ATT_LOOKAHEAD])
            consume((pos0 + n) % ATT_RING, *unit)

    nc = tq // cw
    per_q = tq // tk
    cpb = tk // cw
    units = [(h, c) for c in range(nc) for h in range(2)]
    diag = []
    for r in reversed(range(per_q)):
        for h, c in units:
            if c >= (r + 1) * cpb:
                diag.append((per_q * i + r, h, c, None, tk))
            elif c >= r * cpb:
                diag.append((per_q * i + r, h, c, r * tk, (c - r * cpb + 1) * cw))
    assert nc % ATT_RING == 0 and len(diag) % ATT_RING == 0 and ATT_LOOKAHEAD < ATT_RING <= nc
    assert ATT_BLOCKS_PER_ITER == 1

    split = len(diag) - 2 * ATT_LOOKAHEAD
    assert {(h, c) for _, h, c, _, _ in diag[:split]} == set(units)
    for n in range(ATT_LOOKAHEAD):
        produce(n, *diag[n])
    run(diag[:split], diag[split:], 0)

    n_iters = (per_q // ATT_BLOCKS_PER_ITER) * i
    tiles_per_q = tq // TM_IN
    tiles_per_iter = ATT_BLOCKS_PER_ITER * tk // TM_IN
    n_tiles = st_ref.shape[1]
    trow = lax.broadcasted_iota(jnp.int32, (n_tiles, LANES), 0)
    in_q = (trow >= i * tiles_per_q) & (trow < (i + 1) * tiles_per_q)
    q2 = jnp.max(jnp.where(in_q, st_ref[STAT_Q2], 0.0), axis=0, keepdims=True)
    k2 = jnp.max(st_ref[STAT_K2], axis=0, keepdims=True)
    bound = PRUNE_NORM_SAFETY * jnp.sqrt(q2 * k2)
    c_t0 = jnp.max(jnp.where(trow == i * tiles_per_q, st_ref[STAT_C_FIRST], -jnp.inf),
                   axis=0, keepdims=True)
    head_lane = lax.broadcasted_iota(jnp.int32, (1, LANES), 1) - 2 * pl.program_id(1)
    m_min = jnp.where(head_lane == 0, jnp.min(m_sc[0]),
                      jnp.where(head_lane == 1, jnp.min(m_sc[1]), -jnp.inf))
    prunable = (c_t0 - st_ref[STAT_C_LAST]) + bound <= m_min - PRUNE_MARGIN
    group_end = (trow + 1) % tiles_per_iter == 0
    count = jnp.sum(jnp.where(prunable & group_end, 1.0, 0.0), axis=0, keepdims=True)
    skip = [jnp.minimum(jnp.sum(jnp.where(head_lane == h, count, 0.0)).astype(jnp.int32), n_iters)
            for h in range(2)]
    lo, hi = jnp.minimum(skip[0], skip[1]), jnp.maximum(skip[0], skip[1])
    far = jnp.where(skip[0] <= skip[1], 0, 1)
    one_head = lambda blk, h: [(blk, h, c, None, tk) for c in range(nc)]

    run(diag[split:], one_head(lo, jnp.where(lo < hi, far, 0)), split)

    def body_one(j, carry):
        run(one_head(j, far), one_head(j + 1, jnp.where(j + 1 < hi, far, 0)))
        return carry

    lax.fori_loop(lo, hi, body_one, 0)

    def body_both(j, carry):
        run(one_head(j, 0) + one_head(j, 1), one_head(j + 1, 0))
        return carry

    lax.fori_loop(hi, n_iters, body_both, 0)

    outs = []
    for h in range(2):
        a = acc_sc[h]
        outs.append(a[0:HEAD_DIM, :] / a[HEAD_DIM:HEAD_DIM + 1, :])
    o_ref[...] = jnp.concatenate(outs, axis=0).T.astype(o_ref.dtype)


def _attention(qa, ka, vt, stats):
    b, s, _ = qa.shape
    tq, tk = T_Q, T_K
    return pl.pallas_call(
        _attn_kernel,
        grid=(b, N_PAIRS, s // tq),
        in_specs=[pl.BlockSpec((None, tq, PAIR_WIDTH), lambda bi, p, i: (bi, i, p)),
                  pl.BlockSpec((None, s, PAIR_WIDTH), lambda bi, p, i: (bi, 0, p)),
                  pl.BlockSpec((None, 2, s // tk, VT_ROWS, tk), lambda bi, p, i: (bi, p, 0, 0, 0)),
                  pl.BlockSpec((None,) + stats.shape[1:], lambda bi, p, i: (bi, 0, 0, 0))],
        out_specs=pl.BlockSpec((None, tq, LANES), lambda bi, p, i: (bi, i, p)),
        out_shape=jax.ShapeDtypeStruct((b, s, ATTN_WIDTH), _BF16),
        scratch_shapes=([pltpu.VMEM((2, tq, PAIR_WIDTH), _BF16),
                         pltpu.VMEM((2, 1, tq), _F32),
                         pltpu.VMEM((2, VT_ROWS, tq), _F32)]
                        + [pltpu.VMEM((tk, ATT_CW), _F32) for _ in range(ATT_RING)]
                        + [pltpu.VMEM((1, ATT_CW), _F32) for _ in range(ATT_RING)]),
        compiler_params=pltpu.CompilerParams(
            dimension_semantics=("arbitrary", "arbitrary", "arbitrary"),
            vmem_limit_bytes=VMEM_LIMIT_ATTN),
        name="fox_attn",
    )(qa, ka, vt, stats)


def _layer_norm(z, g, b):
    mu = jnp.mean(z, axis=-1, keepdims=True)
    zc = z - mu
    var = jnp.mean(zc * zc, axis=-1, keepdims=True)
    return zc * lax.rsqrt(var + LN_EPS) * g + b


def _post_kernel(x_ref, pool_ref, attn_ref, wop_ref, woa_ref, g1_ref, b1_ref,
                 w1_ref, w2_ref, g2_ref, b2_ref, o_ref):
    sub = x_ref.shape[0] // POST_SUBTILES
    rows = [slice(n * sub, (n + 1) * sub) for n in range(POST_SUBTILES)]

    def out_proj(r):
        return _dot(pool_ref[r, :], wop_ref[...]) + _dot(attn_ref[r, :], woa_ref[...])

    def mlp_chunks(x1b, y, chunks):
        for c in chunks:
            sl = slice(c * FF_CHUNK, (c + 1) * FF_CHUNK)
            hid = jnp.maximum(_dot(x1b, w1_ref[:, sl]), 0.0)
            y = y + _dot((hid * hid).astype(_BF16), w2_ref[sl, :])
        return y

    def ln1(r, mix):
        return _layer_norm(ALPHA * x_ref[r, :] + mix, g1_ref[...], b1_ref[...])

    def ln2_store(r, x1, y):
        o_ref[r, :] = _layer_norm(ALPHA * x1 + y, g2_ref[...], b2_ref[...])

    first, rest = [0], list(range(1, D_FF // FF_CHUNK))
    mix = [out_proj(r) for r in rows]
    x1 = [ln1(rows[0], mix[0])] + [None] * (len(rows) - 1)
    y_prev = None
    for n, r in enumerate(rows):
        x1n_b = x1[n].astype(_BF16)
        y = mlp_chunks(x1n_b, jnp.zeros_like(x1[n]), first)
        if n + 1 < len(rows):
            x1[n + 1] = ln1(rows[n + 1], mix[n + 1])
        if n > 0:
            ln2_store(rows[n - 1], x1[n - 1], y_prev)
        y_prev = mlp_chunks(x1n_b, y, rest)
    ln2_store(rows[-1], x1[-1], y_prev)


def _post(x, pool, attn, wop, woa, g1, b1, w1, w2, g2, b2):
    t, d = x.shape
    tm = TM_POST
    const = lambda a: pl.BlockSpec(a.shape, lambda i: (0,) * a.ndim, pipeline_mode=pl.Buffered(1))
    tile = lambda w: pl.BlockSpec((tm, w), lambda i: (i, 0))
    return pl.pallas_call(
        _post_kernel,
        grid=(t // tm,),
        in_specs=[tile(d), tile(512), tile(512), const(wop), const(woa), const(g1), const(b1),
                  const(w1), const(w2), const(g2), const(b2)],
        out_specs=tile(d),
        out_shape=jax.ShapeDtypeStruct((t, d), _F32),
        compiler_params=pltpu.CompilerParams(
            dimension_semantics=("arbitrary",), vmem_limit_bytes=VMEM_LIMIT_POST),
        name="post",
    )(x, pool, attn, wop, woa, g1, b1, w1, w2, g2, b2)


def _gate_placement():
    eq = np.zeros((LANES, LANES), np.float32)
    ek = np.zeros((LANES, LANES), np.float32)
    for h in range(N_HEADS):
        base = h * GATE_COLS_PER_HEAD
        for piece in range(3):
            eq[8 * piece + h, base + piece] = 1.0
            eq[GATE_ONE_LANE, base + 3 + piece] = 1.0
            ek[GATE_ONE_LANE, base + piece] = 1.0
            ek[8 * piece + h, base + 3 + piece] = -1.0
    return jnp.asarray(np.concatenate([eq, ek], axis=1), _BF16)


def kernel(x, w_in, b_f, pool_w, pool_scale, w_out, ln1_g, ln1_b, w_mlp1, w_mlp2, ln2_g, ln2_b):
    b, s, d = x.shape
    assert s % T_Q == 0 and (b * s) % TM_POST == 0
    eg = _gate_placement()
    hind = jnp.asarray(np.arange(ATTN_WIDTH)[:, None] // HEAD_DIM == np.arange(LANES)[None, :], _BF16)
    o_q = POOL_WIDTH
    o_k = o_q + ATTN_WIDTH
    o_v = o_k + ATTN_WIDTH
    o_f = o_v + ATTN_WIDTH
    row = lambda a: a.reshape(1, -1)
    for l in range(DEPTH):
        w = w_in[l]
        wu = w[:, :o_q].astype(_BF16)
        wq = (w[:, o_q:o_k] * (HEAD_DIM ** -0.5 * LOG2E)).astype(_BF16)
        wk = w[:, o_k:o_v].astype(_BF16)
        wvt = w[:, o_v:o_f].T.astype(_BF16)
        wf = jnp.pad(jnp.tile(w[:, o_f:], (1, GATE_COPIES)),
                     ((0, 0), (0, LANES - GATE_COPIES * N_HEADS))).astype(_BF16)
        bf = jnp.pad(jnp.tile(b_f[l], GATE_COPIES), (0, LANES - GATE_COPIES * N_HEADS)).reshape(1, LANES)
        pool, qa, ka, vt, stats = _inproj(x, wu, wq, wk, wvt, wf, bf, pool_w[l].astype(_BF16),
                                          row(pool_scale[l]), eg, hind)
        attn = _attention(qa, ka, vt, stats.transpose(0, 2, 1, 3))
        wo = w_out[l].astype(_BF16)
        x = _post(x.reshape(b * s, d), pool.reshape(b * s, -1), attn.reshape(b * s, -1),
                  wo[:POOL_WIDTH], wo[POOL_WIDTH:], row(ln1_g[l]), row(ln1_b[l]),
                  w_mlp1[l].astype(_BF16), w_mlp2[l].astype(_BF16),
                  row(ln2_g[l]), row(ln2_b[l])).reshape(b, s, d)
    return x
```
